```python
import math, functools
import jax, jax.numpy as jnp
from jax import lax
import numpy as np

D_MODEL = 1024
BATCH = 2
SEQ = 8192
DEPTH = 1
DEC_BATCH = 128
DEC_SEQ = 4
PAST_LEN = 8192
PAGE_SIZE = 128

N_META = 16
ATT_HEADS = 4
HEAD_QK = 64
HEAD_V = 2 * HEAD_QK
ATT_WIDTH = ATT_HEADS * HEAD_V
Q_WIDTH = ATT_HEADS * 2 * HEAD_QK
ROPE_DIM = HEAD_QK // 4
ROPE_THETA = 500000.0
RNN_WIDTH = D_MODEL - ATT_WIDTH
RNN_BLOCKS = 8
RNN_BLOCK = RNN_WIDTH // RNN_BLOCKS
CONV_W = 4
LRU_C = 8.0
IN_WIDTH = 2 * Q_WIDTH + ATT_WIDTH + 2 * RNN_WIDTH
N_GROUPS = 4
EXP_PER_GROUP = 4
N_EXPERTS = N_GROUPS * EXP_PER_GROUP
TOP_K = 2
D_EXPERT = D_MODEL // 2

Q_BLOCK = 128
NORM_EPS = 1e-6
NEG_INF = -1e30

kernel_name = 'hymba_diffattn_rglru_hiermoe_step'


def rmsnorm(x, g):
    xf = x.astype(jnp.float32)
    y = xf * lax.rsqrt(jnp.mean(xf * xf, axis=-1, keepdims=True) + NORM_EPS) * g.astype(jnp.float32)
    return y.astype(x.dtype)


def rope(x, pos):
    half = ROPE_DIM // 2
    inv = jnp.power(ROPE_THETA, -2.0 * jnp.arange(half, dtype=jnp.float32) / ROPE_DIM)
    ang = pos.astype(jnp.float32)[:, None] * inv[None, :]
    cos = jnp.cos(ang)[None, :, None, None, :]
    sin = jnp.sin(ang)[None, :, None, None, :]
    xf = x.astype(jnp.float32)
    x1 = xf[..., :half]
    x2 = xf[..., half:ROPE_DIM]
    out = jnp.concatenate([x1 * cos - x2 * sin, x2 * cos + x1 * sin, xf[..., ROPE_DIM:]], axis=-1)
    return out.astype(x.dtype)


def diff_attn_core(q, k, v, mask, lam):
    s = jnp.einsum('bqhcd,bkhcd->bhcqk', q.astype(jnp.float32), k.astype(jnp.float32)) * (HEAD_QK ** -0.5)
    s = jnp.where(mask, s, NEG_INF)
    p = jax.nn.softmax(s, axis=-1)
    w = p[:, :, 0] - lam * p[:, :, 1]
    return jnp.einsum('bhqk,bkhd->bqhd', w, v.astype(jnp.float32))


def diff_head_out(o, gain, lam_init, dtype):
    o = o * lax.rsqrt(jnp.mean(o * o, axis=-1, keepdims=True) + NORM_EPS) * gain.astype(jnp.float32)
    o = o * (1.0 - lam_init)
    b, t = o.shape[0], o.shape[1]
    return o.reshape(b, t, ATT_WIDTH).astype(dtype)


def prompt_diff_attention(q, k, v, lam, gain, lam_init):
    b, L = q.shape[0], q.shape[1]
    front = (-N_META) % Q_BLOCK
    back = (-(L + front)) % Q_BLOCK
    def pad(t):
        return jnp.pad(t, [(0, 0), (front, back)] + [(0, 0)] * (t.ndim - 2))
    qp, kp, vp = pad(q), pad(k), pad(v)
    Lp = L + front + back
    kpos = jnp.arange(Lp)
    def block(i):
        start = i * Q_BLOCK
        qb = lax.dynamic_slice_in_dim(qp, start, Q_BLOCK, axis=1)
        qpos = start + jnp.arange(Q_BLOCK)
        mask = (kpos[None, :] <= qpos[:, None]) & (kpos[None, :] >= front)
        return diff_attn_core(qb, kp, vp, mask, lam)
    o = lax.map(block, jnp.arange(Lp // Q_BLOCK))
    o = jnp.moveaxis(o, 0, 1).reshape(b, Lp, ATT_HEADS, HEAD_V)[:, front:front + L]
    return diff_head_out(o, gain, lam_init, q.dtype)


def paged_diff_attention(q, k, v, cache_k, cache_v, page_table, layer, lam, gain, lam_init):
    t = q.shape[1]
    past_len = page_table.shape[1] * PAGE_SIZE
    mask = jnp.concatenate([jnp.ones((t, past_len), bool), jnp.tril(jnp.ones((t, t), bool))], axis=1)
    def one(args):
        qs, ks, vs, pages = args
        pk = cache_k[layer, pages].reshape(past_len, ATT_HEADS, 2, HEAD_QK)
        pv = cache_v[layer, pages].reshape(past_len, ATT_HEADS, HEAD_V)
        kk = jnp.concatenate([pk.astype(ks.dtype), ks], axis=0)
        vv = jnp.concatenate([pv.astype(vs.dtype), vs], axis=0)
        return diff_attn_core(qs[None], kk[None], vv[None], mask, lam)[0]
    o = lax.map(one, (q, k, v, page_table))
    return diff_head_out(o, gain, lam_init, q.dtype)


def causal_conv(x, buf, w, b):
    t = x.shape[1]
    xp = jnp.concatenate([buf.astype(x.dtype), x], axis=1)
    y = b + xp[:, 0:t] * w[0]
    for j in range(1, CONV_W):
        y = y + xp[:, j:j + t] * w[j]
    return y, xp[:, -(CONV_W - 1):]


def linear_scan(a, u, h0):
    def step(h, au):
        at, ut = au
        h = at * h + ut
        return h, h
    h_last, hs = lax.scan(step, h0, (jnp.swapaxes(a, 0, 1), jnp.swapaxes(u, 0, 1)))
    return jnp.swapaxes(hs, 0, 1), h_last


def rg_lru(x, h0, w_a, b_a, w_i, b_i, lru_lambda):
    b, t, r = x.shape
    xb = x.reshape(b, t, RNN_BLOCKS, RNN_BLOCK)
    gate_r = jax.nn.sigmoid((jnp.einsum('btnk,nkj->btnj', xb, w_a).reshape(b, t, r) + b_a).astype(jnp.float32))
    gate_i = jax.nn.sigmoid((jnp.einsum('btnk,nkj->btnj', xb, w_i).reshape(b, t, r) + b_i).astype(jnp.float32))
    log_a = -LRU_C * gate_r * jax.nn.softplus(-lru_lambda.astype(jnp.float32))
    a = jnp.exp(log_a)
    u = jnp.sqrt(-jnp.expm1(2.0 * log_a)) * (gate_i * x.astype(jnp.float32))
    return linear_scan(a, u, h0.astype(jnp.float32))


def token_mixers(hn, pos, attend, conv_buf, h0, w_in, conv_w, conv_b, w_a, b_a, w_i, b_i, lru_lambda, w_out):
    b, t, _ = hn.shape
    proj = hn @ w_in
    q, k, v, xr, xg = jnp.split(proj, [Q_WIDTH, 2 * Q_WIDTH, 2 * Q_WIDTH + ATT_WIDTH,
                                       2 * Q_WIDTH + ATT_WIDTH + RNN_WIDTH], axis=-1)
    q = rope(q.reshape(b, t, ATT_HEADS, 2, HEAD_QK), pos)
    k = rope(k.reshape(b, t, ATT_HEADS, 2, HEAD_QK), pos)
    v = v.reshape(b, t, ATT_HEADS, HEAD_V)
    att = attend(q, k, v)
    xc, new_buf = causal_conv(xr, conv_buf, conv_w, conv_b)
    hr, h_last = rg_lru(xc, h0, w_a, b_a, w_i, b_i, lru_lambda)
    rnn = jax.nn.gelu(xg) * hr.astype(xg.dtype)
    out = jnp.concatenate([att, rnn], axis=-1) @ w_out
    return out, k, v, new_buf, h_last


def hier_moe(x, w_grp, b_grp, w_rt, b_rt, w_gate, w_up, w_down):
    b, t, d = x.shape
    xt = x.reshape(b * t, d)
    g_logit = (xt @ w_grp).astype(jnp.float32) + b_grp.astype(jnp.float32)
    g_prob = jax.nn.softmax(g_logit, axis=-1)
    g_idx = jnp.argmax(g_logit, axis=-1)
    g_sel = jnp.take_along_axis(g_prob, g_idx[:, None], axis=-1)
    e_logit = ((xt @ w_rt).astype(jnp.float32) + b_rt.astype(jnp.float32)).reshape(-1, N_GROUPS, EXP_PER_GROUP)
    e_in = jnp.take_along_axis(e_logit, g_idx[:, None, None], axis=1)[:, 0]
    top_v, top_i = lax.top_k(e_in, TOP_K)
    wts = jax.nn.softmax(top_v, axis=-1) * g_sel
    eid = g_idx[:, None] * EXP_PER_GROUP + top_i
    gates = jnp.einsum('nk,nke->ne', wts, jax.nn.one_hot(eid, N_EXPERTS, dtype=jnp.float32))
    out = jnp.zeros((b * t, d), jnp.float32)
    for e in range(N_EXPERTS):
        he = jax.nn.silu(xt @ w_gate[e]) * (xt @ w_up[e])
        out = out + gates[:, e:e + 1] * (he @ w_down[e]).astype(jnp.float32)
    return out.astype(x.dtype).reshape(b, t, d)


def setup_inputs(seed: int = 0) -> dict:
    key = jax.random.key(seed)
    ks = jax.random.split(key, 32)
    f32 = jnp.float32
    n_pages = PAST_LEN // PAGE_SIZE
    n_used = DEC_BATCH * n_pages
    n_pool = n_used + max(1, n_used // 4)
    def nrm(k, shape, scale):
        return scale * jax.random.normal(k, shape, f32)
    a_base = jax.random.uniform(ks[0], (DEPTH, RNN_WIDTH), f32, 0.9, 0.999)
    p = a_base ** (1.0 / LRU_C)
    lru_lambda = jnp.log(p) - jnp.log1p(-p)
    page_table = jax.random.permutation(ks[1], n_pool)[:n_used].reshape(DEC_BATCH, n_pages).astype(jnp.int32)
    return {
        'x_prompt': nrm(ks[2], (BATCH, SEQ, D_MODEL), 1.0),
        'x_sample': nrm(ks[3], (DEC_BATCH, DEC_SEQ, D_MODEL), 1.0),
        'cache_k': nrm(ks[4], (DEPTH, n_pool, PAGE_SIZE, ATT_HEADS, 2, HEAD_QK), 1.0),
        'cache_v': nrm(ks[5], (DEPTH, n_pool, PAGE_SIZE, ATT_HEADS, HEAD_V), 1.0),
        'state_conv': nrm(ks[6], (DEPTH, DEC_BATCH, CONV_W - 1, RNN_WIDTH), 1.0),
        'state_h': nrm(ks[7], (DEPTH, DEC_BATCH, RNN_WIDTH), 0.5),
        'page_table': page_table,
        'meta_tokens': nrm(ks[8], (N_META, D_MODEL), 1.0),
        'norm1': 1.0 + nrm(ks[9], (DEPTH, D_MODEL), 0.01),
        'w_in': nrm(ks[10], (DEPTH, D_MODEL, IN_WIDTH), D_MODEL ** -0.5),
        'lambda_q1': nrm(ks[11], (DEPTH, HEAD_QK), 0.1),
        'lambda_k1': nrm(ks[12], (DEPTH, HEAD_QK), 0.1),
        'lambda_q2': nrm(ks[13], (DEPTH, HEAD_QK), 0.1),
        'lambda_k2': nrm(ks[14], (DEPTH, HEAD_QK), 0.1),
        'subln': 1.0 + nrm(ks[15], (DEPTH, HEAD_V), 0.01),
        'conv_w': nrm(ks[16], (DEPTH, CONV_W, RNN_WIDTH), CONV_W ** -0.5),
        'conv_b': nrm(ks[17], (DEPTH, RNN_WIDTH), 0.01),
        'rg_w_a': nrm(ks[18], (DEPTH, RNN_BLOCKS, RNN_BLOCK, RNN_BLOCK), RNN_BLOCK ** -0.5),
        'rg_b_a': nrm(ks[19], (DEPTH, RNN_WIDTH), 0.01),
        'rg_w_i': nrm(ks[20], (DEPTH, RNN_BLOCKS, RNN_BLOCK, RNN_BLOCK), RNN_BLOCK ** -0.5),
        'rg_b_i': nrm(ks[21], (DEPTH, RNN_WIDTH), 0.01),
        'lru_lambda': lru_lambda,
        'w_out': nrm(ks[22], (DEPTH, D_MODEL, D_MODEL), D_MODEL ** -0.5),
        'norm2': 1.0 + nrm(ks[23], (DEPTH, D_MODEL), 0.01),
        'w_grp': nrm(ks[24], (DEPTH, D_MODEL, N_GROUPS), D_MODEL ** -0.5),
        'b_grp': nrm(ks[25], (DEPTH, N_GROUPS), 0.01),
        'w_rt': nrm(ks[26], (DEPTH, D_MODEL, N_EXPERTS), D_MODEL ** -0.5),
        'b_rt': nrm(ks[27], (DEPTH, N_EXPERTS), 0.01),
        'w_gate': nrm(ks[28], (DEPTH, N_EXPERTS, D_MODEL, D_EXPERT), D_MODEL ** -0.5),
        'w_up': nrm(ks[29], (DEPTH, N_EXPERTS, D_MODEL, D_EXPERT), D_MODEL ** -0.5),
        'w_down': nrm(ks[30], (DEPTH, N_EXPERTS, D_EXPERT, D_MODEL), D_EXPERT ** -0.5),
        'norm_f': 1.0 + nrm(ks[31], (D_MODEL,), 0.01),
    }


def reference(x_prompt, x_sample, cache_k, cache_v, state_conv, state_h, page_table, meta_tokens,
              norm1, w_in, lambda_q1, lambda_k1, lambda_q2, lambda_k2, subln, conv_w, conv_b,
              rg_w_a, rg_b_a, rg_w_i, rg_b_i, lru_lambda, w_out, norm2, w_grp, b_grp, w_rt, b_rt,
              w_gate, w_up, w_down, norm_f):
    f32 = jnp.float32
    b = x_prompt.shape[0]
    hp = jnp.concatenate([jnp.broadcast_to(meta_tokens.astype(x_prompt.dtype)[None], (b, N_META, D_MODEL)),
                          x_prompt], axis=1)
    hs = x_sample
    past_len = page_table.shape[1] * PAGE_SIZE
    pos_p = jnp.arange(hp.shape[1])
    pos_s = past_len + jnp.arange(hs.shape[1])
    kp_all, vp_all, ks_all, vs_all, cp_all, cs_all, hp_all, hs_all = [], [], [], [], [], [], [], []
    for l in range(DEPTH):
        lam_init = 0.8 - 0.6 * math.exp(-0.3 * l)
        lam = (jnp.exp(jnp.sum(lambda_q1[l].astype(f32) * lambda_k1[l].astype(f32)))
               - jnp.exp(jnp.sum(lambda_q2[l].astype(f32) * lambda_k2[l].astype(f32))) + lam_init)
        rnn_w = (conv_w[l], conv_b[l], rg_w_a[l], rg_b_a[l], rg_w_i[l], rg_b_i[l], lru_lambda[l])
        moe_w = (w_grp[l], b_grp[l], w_rt[l], b_rt[l], w_gate[l], w_up[l], w_down[l])
        att_p = functools.partial(prompt_diff_attention, lam=lam, gain=subln[l], lam_init=lam_init)
        mix, k_new, v_new, cbuf, h_last = token_mixers(
            rmsnorm(hp, norm1[l]), pos_p, att_p,
            jnp.zeros((b, CONV_W - 1, RNN_WIDTH), hp.dtype), jnp.zeros((b, RNN_WIDTH), f32),
            w_in[l], *rnn_w, w_out[l])
        hp = hp + mix
        hp = hp + hier_moe(rmsnorm(hp, norm2[l]), *moe_w)
        kp_all.append(k_new); vp_all.append(v_new); cp_all.append(cbuf); hp_all.append(h_last)
        att_s = functools.partial(paged_diff_attention, cache_k=cache_k, cache_v=cache_v, page_table=page_table,
                                  layer=l, lam=lam, gain=subln[l], lam_init=lam_init)
        mix, k_new, v_new, cbuf, h_last = token_mixers(
            rmsnorm(hs, norm1[l]), pos_s, att_s, state_conv[l], state_h[l],
            w_in[l], *rnn_w, w_out[l])
        hs = hs + mix
        hs = hs + hier_moe(rmsnorm(hs, norm2[l]), *moe_w)
        ks_all.append(k_new); vs_all.append(v_new); cs_all.append(cbuf); hs_all.append(h_last)
    y_prompt = rmsnorm(hp, norm_f)[:, N_META:]
    y_sample = rmsnorm(hs, norm_f)
    return (y_prompt, y_sample, jnp.stack(kp_all), jnp.stack(vp_all), jnp.stack(ks_all), jnp.stack(vs_all),
            jnp.stack(cp_all), jnp.stack(cs_all), jnp.stack(hp_all), jnp.stack(hs_all))
```

```python
import functools
import math

import jax
import jax.numpy as jnp
from jax import lax
from jax.experimental import pallas as pl
from jax.experimental.pallas import tpu as pltpu

F32 = jnp.float32
BF16 = jnp.bfloat16

N_META = 16
ATT_HEADS = 4
HEAD_QK = 64
HEAD_V = 2 * HEAD_QK
ATT_WIDTH = ATT_HEADS * HEAD_V
Q_WIDTH = ATT_HEADS * 2 * HEAD_QK
ROPE_DIM = HEAD_QK // 4
ROPE_THETA = 500000.0
RNN_BLOCKS = 8
CONV_W = 4
LRU_C = 8.0
N_GROUPS = 4
EXP_PER_GROUP = 4
N_EXPERTS = N_GROUPS * EXP_PER_GROUP
NORM_EPS = 1e-6
NEG_INF = -1e30
LAM_INIT = 0.8 - 0.6 * math.exp(-0.3 * 0)

LANES = 128
ROW_TILE = 256
VMEM_LIMIT = 56 * 1024 * 1024


def _params(sem, vmem=VMEM_LIMIT):
    return pltpu.CompilerParams(dimension_semantics=sem, vmem_limit_bytes=vmem)


def _sigmoid(x):
    return 1.0 / (1.0 + jnp.exp(-x))


def _softplus(x):
    return jnp.maximum(x, 0.0) + jnp.log1p(jnp.exp(-jnp.abs(x)))


def _gelu_tanh(x):
    return 0.5 * x * (1.0 + jnp.tanh(math.sqrt(2.0 / math.pi) * (x + 0.044715 * (x * x * x))))


def _lam_scalar(lamv):
    s1 = jnp.sum(lamv[0:1] * lamv[1:2], axis=-1, keepdims=True)
    s2 = jnp.sum(lamv[2:3] * lamv[3:4], axis=-1, keepdims=True)
    return jnp.exp(s1) - jnp.exp(s2) + LAM_INIT


def _inproj_body(x_ref, g_ref, w_ref, cs_ref, q_ref, k_ref, kb_ref, v_ref, vb_ref, xr_ref, xg_ref):
    x = x_ref[...]
    ms = jnp.mean(x * x, axis=-1, keepdims=True)
    hn = (x * lax.rsqrt(ms + NORM_EPS) * g_ref[...]).astype(BF16)
    proj = jnp.dot(hn, w_ref[...], preferred_element_type=F32)
    ct, s1, s2 = cs_ref[0], cs_ref[1], cs_ref[2]

    def rope(xc):
        return xc * ct + pltpu.roll(xc, LANES - ROPE_DIM // 2, 1) * s1 + pltpu.roll(xc, ROPE_DIM // 2, 1) * s2

    for c in range(Q_WIDTH // LANES):
        sl = slice(c * LANES, (c + 1) * LANES)
        qc = rope(proj[:, c * LANES:(c + 1) * LANES])
        q_ref[:, sl] = (qc * (HEAD_QK ** -0.5)).astype(BF16)
        kc = rope(proj[:, Q_WIDTH + c * LANES:Q_WIDTH + (c + 1) * LANES])
        k_ref[:, sl] = kc
        kb_ref[:, sl] = kc.astype(BF16)
    v = proj[:, 2 * Q_WIDTH:2 * Q_WIDTH + ATT_WIDTH]
    v_ref[...] = v
    vb_ref[...] = v.astype(BF16)
    r0 = 2 * Q_WIDTH + ATT_WIDTH
    rw = xr_ref.shape[1]
    xr_ref[...] = proj[:, r0:r0 + rw]
    xg_ref[...] = proj[:, r0 + rw:r0 + 2 * rw]


def _inproj(x_all, norm1, w_in_b, cs):
    nt, d = x_all.shape
    rw = (w_in_b.shape[1] - 2 * Q_WIDTH - ATT_WIDTH) // 2
    tm = ROW_TILE
    row = lambda i: (i, 0)
    const = lambda i: (0, 0)
    outs = [
        jax.ShapeDtypeStruct((nt, Q_WIDTH), BF16),
        jax.ShapeDtypeStruct((nt, Q_WIDTH), F32),
        jax.ShapeDtypeStruct((nt, Q_WIDTH), BF16),
        jax.ShapeDtypeStruct((nt, ATT_WIDTH), F32),
        jax.ShapeDtypeStruct((nt, ATT_WIDTH), BF16),
        jax.ShapeDtypeStruct((nt, rw), F32),
        jax.ShapeDtypeStruct((nt, rw), F32),
    ]
    return pl.pallas_call(
        _inproj_body,
        grid=(nt // tm,),
        in_specs=[
            pl.BlockSpec((tm, d), row),
            pl.BlockSpec((1, d), const),
            pl.BlockSpec(w_in_b.shape, const),
            pl.BlockSpec((3, tm, LANES), lambda i: (0, i, 0)),
        ],
        out_specs=[pl.BlockSpec((tm, o.shape[1]), row) for o in outs],
        out_shape=outs,
        compiler_params=_params(("parallel",)),
        name="inproj",
    )(x_all, norm1, w_in_b, cs)


def _flash_body(q_ref, k_ref, v_ref, lamv_ref, gain_ref, o_ref, m_sc, l_sc, acc_sc, *, tq, front):
    qi = pl.program_id(2)
    q = q_ref[...]
    lane = lax.broadcasted_iota(jnp.int32, q.shape, 1)
    zero = jnp.zeros_like(q)
    q2 = jnp.concatenate([jnp.where(lane < HEAD_QK, q, zero), jnp.where(lane >= HEAD_QK, q, zero)], axis=0)
    m_sc[...] = jnp.full(m_sc.shape, NEG_INF, F32)
    l_sc[...] = jnp.zeros(l_sc.shape, F32)
    acc_sc[...] = jnp.zeros(acc_sc.shape, F32)

    def step(ki, masked):
        start = pl.multiple_of(ki * tq, tq)
        k = k_ref[pl.ds(start, tq), :]
        v = v_ref[pl.ds(start, tq), :]
        s = lax.dot_general(q2, k, (((1,), (1,)), ((), ())), preferred_element_type=F32)
        if masked:
            row = lax.broadcasted_iota(jnp.int32, s.shape, 0)
            col = lax.broadcasted_iota(jnp.int32, s.shape, 1)
            qpos = qi * tq + jnp.where(row >= tq, row - tq, row)
            kpos = ki * tq + col
            s = jnp.where(kpos <= qpos, jnp.where(kpos >= front, s, NEG_INF), NEG_INF)
        m_prev = m_sc[...]
        m_next = jnp.maximum(m_prev, jnp.max(s, axis=1, keepdims=True))
        p = jnp.exp(s - m_next[:, 0:1])
        alpha = jnp.exp(m_prev - m_next)
        l_sc[...] = alpha * l_sc[...] + jnp.sum(p, axis=1, keepdims=True)
        acc_sc[...] = alpha * acc_sc[...] + jnp.dot(p.astype(BF16), v, preferred_element_type=F32)
        m_sc[...] = m_next

    step(0, True)

    def body(ki, carry):
        step(ki, False)
        return carry

    lax.fori_loop(1, qi, body, 0)

    @pl.when(qi > 0)
    def _():
        step(qi, True)

    o = acc_sc[...] / l_sc[...]
    d = o[:tq] - _lam_scalar(lamv_ref[...]) * o[tq:]
    ms = jnp.mean(d * d, axis=-1, keepdims=True)
    o_ref[...] = (d * lax.rsqrt(ms + NORM_EPS) * gain_ref[...] * (1.0 - LAM_INIT)).astype(BF16)


def _prompt_attention(q_all, kb_all, vb_all, lamv, gain, n_batch, lp, front):
    tq = ROW_TILE
    nq = lp // tq
    kern = functools.partial(_flash_body, tq=tq, front=front)
    return pl.pallas_call(
        kern,
        grid=(n_batch, ATT_HEADS, nq),
        in_specs=[
            pl.BlockSpec((tq, LANES), lambda b, h, i: (b * nq + i, h)),
            pl.BlockSpec((lp, LANES), lambda b, h, i: (b, h)),
            pl.BlockSpec((lp, LANES), lambda b, h, i: (b, h)),
            pl.BlockSpec(lamv.shape, lambda b, h, i: (0, 0)),
            pl.BlockSpec(gain.shape, lambda b, h, i: (0, 0)),
        ],
        out_specs=pl.BlockSpec((tq, LANES), lambda b, h, i: (b * nq + i, h)),
        out_shape=jax.ShapeDtypeStruct((n_batch * lp, ATT_WIDTH), BF16),
        scratch_shapes=[
            pltpu.VMEM((2 * tq, LANES), F32),
            pltpu.VMEM((2 * tq, LANES), F32),
            pltpu.VMEM((2 * tq, HEAD_V), F32),
        ],
        compiler_params=_params(("parallel", "parallel", "arbitrary")),
        name="prompt_attn",
    )(q_all, kb_all, vb_all, lamv, gain)


def _paged_body(pt_ref, qbd_ref, kn_ref, vn_ref, lamv_ref, gain_ref, *rest, n_pg, n_chunks, n_t):
    del pt_ref
    k_refs = rest[:n_pg]
    v_refs = rest[n_pg:2 * n_pg]
    o_ref = rest[2 * n_pg]
    m_sc, l_sc, acc_sc = rest[2 * n_pg + 1:]
    j = pl.program_id(1)
    n_rows = qbd_ref.shape[0]

    @pl.when(j == 0)
    def _():
        m_sc[...] = jnp.full(m_sc.shape, NEG_INF, F32)
        l_sc[...] = jnp.zeros(l_sc.shape, F32)
        acc_sc[...] = jnp.zeros(acc_sc.shape, F32)

    qbd = qbd_ref[...]
    dn = (((1,), (1,)), ((), ()))
    s = jnp.concatenate(
        [lax.dot_general(qbd, k_refs[i][...].astype(BF16), dn, preferred_element_type=F32) for i in range(n_pg)],
        axis=1)
    m_prev = m_sc[...]
    m_next = jnp.maximum(m_prev, jnp.max(s, axis=1, keepdims=True))
    p = jnp.exp(s - m_next[:, 0:1])
    alpha = jnp.exp(m_prev - m_next)
    l_sc[...] = alpha * l_sc[...] + jnp.sum(p, axis=1, keepdims=True)
    pb = p.astype(BF16)
    pv = jnp.dot(pb[:, 0:LANES], v_refs[0][...].astype(BF16), preferred_element_type=F32)
    for i in range(1, n_pg):
        pv = pv + jnp.dot(pb[:, i * LANES:(i + 1) * LANES], v_refs[i][...].astype(BF16), preferred_element_type=F32)
    acc_sc[...] = acc_sc[...] * alpha[:, 0:1] + pv
    m_sc[...] = m_next

    @pl.when(j == n_chunks - 1)
    def _():
        qf = qbd.astype(F32)
        kn = kn_ref[...].astype(BF16).astype(F32)
        vn = vn_ref[...].astype(BF16).astype(F32)
        row = lax.broadcasted_iota(jnp.int32, (n_rows, 1), 0)
        row_t = row % n_t
        s_new = []
        for t2 in range(n_t):
            st = jnp.sum(qf * kn[t2:t2 + 1, :], axis=1, keepdims=True)
            s_new.append(jnp.where(row_t >= t2, st, NEG_INF))
        m_old = m_sc[...][:, 0:1]
        m_fin = m_old
        for st in s_new:
            m_fin = jnp.maximum(m_fin, st)
        a_fin = jnp.exp(m_old - m_fin)
        l_fin = a_fin * l_sc[...][:, 0:1]
        acc = acc_sc[...] * a_fin
        for t2 in range(n_t):
            pt = jnp.exp(s_new[t2] - m_fin)
            l_fin = l_fin + pt
            acc = acc + pt * vn[t2:t2 + 1, :]
        o = acc / l_fin
        lane_h = lax.broadcasted_iota(jnp.int32, o.shape, 1) // HEAD_V
        row_h = lax.broadcasted_iota(jnp.int32, o.shape, 0) // (2 * n_t)
        o = jnp.where(lane_h == row_h, o, 0.0)
        lam = _lam_scalar(lamv_ref[...])
        att = jnp.zeros((n_t, ATT_WIDTH), F32)
        for h in range(ATT_HEADS):
            r0 = h * 2 * n_t
            att = att + (o[r0:r0 + n_t] - lam * o[r0 + n_t:r0 + 2 * n_t])
        gain = gain_ref[...]
        for h in range(ATT_HEADS):
            seg = att[:, h * HEAD_V:(h + 1) * HEAD_V]
            ms = jnp.mean(seg * seg, axis=-1, keepdims=True)
            o_ref[:, h * HEAD_V:(h + 1) * HEAD_V] = seg * lax.rsqrt(ms + NORM_EPS) * gain * (1.0 - LAM_INIT)


def _sample_attention(qbd, k_new, v_new, cache_k, cache_v, page_table, lamv, gain):
    n_seq, n_rows, kw = qbd.shape
    n_t = k_new.shape[1]
    n_pages = page_table.shape[1]
    n_pg = math.gcd(n_pages, 16)
    n_chunks = n_pages // n_pg
    pt_flat = page_table.reshape(-1)

    def page_map(i):
        return lambda b, j, pt: (pt[b * n_pages + j * n_pg + i], 0, 0)

    seq3 = lambda b, j, pt: (b, 0, 0)
    const2 = lambda b, j, pt: (0, 0)
    page_block = (None, cache_k.shape[1], kw)
    kern = functools.partial(_paged_body, n_pg=n_pg, n_chunks=n_chunks, n_t=n_t)
    grid_spec = pltpu.PrefetchScalarGridSpec(
        num_scalar_prefetch=1,
        grid=(n_seq, n_chunks),
        in_specs=[
            pl.BlockSpec((None, n_rows, kw), seq3),
            pl.BlockSpec((None, n_t, kw), seq3),
            pl.BlockSpec((None, n_t, kw), seq3),
            pl.BlockSpec(lamv.shape, const2),
            pl.BlockSpec(gain.shape, const2),
        ] + [pl.BlockSpec(page_block, page_map(i)) for i in range(n_pg)]
          + [pl.BlockSpec(page_block, page_map(i)) for i in range(n_pg)],
        out_specs=pl.BlockSpec((None, n_t, ATT_WIDTH), seq3),
        scratch_shapes=[
            pltpu.VMEM((n_rows, LANES), F32),
            pltpu.VMEM((n_rows, LANES), F32),
            pltpu.VMEM((n_rows, ATT_WIDTH), F32),
        ],
    )
    return pl.pallas_call(
        kern,
        grid_spec=grid_spec,
        out_shape=jax.ShapeDtypeStruct((n_seq, n_t, ATT_WIDTH), F32),
        compiler_params=_params(("parallel", "arbitrary")),
        name="sample_attn",
    )(pt_flat, qbd, k_new, v_new, lamv, gain, *([cache_k] * n_pg), *([cache_v] * n_pg))


def _lru_gates(xc, wa_ref, ba, wi_ref, bi, lam_row):
    half = xc.shape[1] // 2
    xb = xc.astype(BF16)

    def blockdiag(w_ref):
        return jnp.concatenate([
            jnp.dot(xb[:, :half], w_ref[0], preferred_element_type=F32),
            jnp.dot(xb[:, half:], w_ref[1], preferred_element_type=F32)], axis=1)

    gate_r = _sigmoid(blockdiag(wa_ref) + ba)
    gate_i = _sigmoid(blockdiag(wi_ref) + bi)
    log_a = -LRU_C * gate_r * _softplus(-lam_row)
    a = jnp.exp(log_a)
    th = jnp.tanh(log_a)
    u = jnp.sqrt(-2.0 * th / (1.0 - th)) * (gate_i * xc)
    return a, u


def _rglru_prompt_body(xr_ref, xg_ref, cw_ref, cb_ref, wa_ref, ba_ref, wi_ref, bi_ref, lam_ref,
                       rnn_ref, hl_ref, prev_sc, h_sc, *, tc, front):
    c = pl.program_id(1)

    @pl.when(c == 0)
    def _():
        prev_sc[...] = jnp.zeros(prev_sc.shape, F32)
        h_sc[...] = jnp.zeros(h_sc.shape, F32)

    x = xr_ref[...]
    xfull = jnp.concatenate([prev_sc[...], x], axis=0)
    cw = cw_ref[...]
    y = cb_ref[...] + xfull[5:5 + tc] * cw[0:1]
    y = y + xfull[6:6 + tc] * cw[1:2]
    y = y + xfull[7:7 + tc] * cw[2:3]
    y = y + x * cw[3:4]
    prev_sc[...] = x[tc - 8:tc]
    a, u = _lru_gates(y, wa_ref, ba_ref[...], wi_ref, bi_ref[...], lam_ref[...])
    row = lax.broadcasted_iota(jnp.int32, (tc, 1), 0)
    u = jnp.where(c * tc + row >= front, u, 0.0)
    d = 1
    while d < tc:
        keep = row >= d
        u = jnp.where(keep, a * pltpu.roll(u, d, 0) + u, u)
        a = jnp.where(keep, a * pltpu.roll(a, d, 0), a)
        d *= 2
    h = a * h_sc[0:1, :] + u
    h_sc[...] = h[tc - 8:tc]
    h_sc[0:1, :] = h[tc - 1:tc]
    hl_ref[...] = h[tc - 8:tc]
    rnn_ref[...] = (_gelu_tanh(xg_ref[...]) * h).astype(BF16)


def _rglru_prompt(xr_all, xg_all, cw, cb, wa_bd, ba, wi_bd, bi, lam, n_batch, lp, front):
    tc = ROW_TILE
    nc = lp // tc
    rw = xr_all.shape[1]
    kern = functools.partial(_rglru_prompt_body, tc=tc, front=front)
    row = lambda b, c: (b * nc + c, 0)
    const2 = lambda b, c: (0, 0)
    const3 = lambda b, c: (0, 0, 0)
    return pl.pallas_call(
        kern,
        grid=(n_batch, nc),
        in_specs=[
            pl.BlockSpec((tc, rw), row),
            pl.BlockSpec((tc, rw), row),
            pl.BlockSpec(cw.shape, const2),
            pl.BlockSpec(cb.shape, const2),
            pl.BlockSpec(wa_bd.shape, const3),
            pl.BlockSpec(ba.shape, const2),
            pl.BlockSpec(wi_bd.shape, const3),
            pl.BlockSpec(bi.shape, const2),
            pl.BlockSpec(lam.shape, const2),
        ],
        out_specs=[
            pl.BlockSpec((tc, rw), row),
            pl.BlockSpec((8, rw), lambda b, c: (b, 0)),
        ],
        out_shape=[
            jax.ShapeDtypeStruct((n_batch * lp, rw), BF16),
            jax.ShapeDtypeStruct((n_batch * 8, rw), F32),
        ],
        scratch_shapes=[pltpu.VMEM((8, rw), F32), pltpu.VMEM((8, rw), F32)],
        compiler_params=_params(("parallel", "arbitrary")),
        name="rglru_prompt",
    )(xr_all, xg_all, cw, cb, wa_bd, ba, wi_bd, bi, lam)


def _rglru_sample_body(xr_ref, xg_ref, cs_ref, h0_ref, cw_ref, cb_ref, wa_ref, ba_ref, wi_ref, bi_ref, lam_ref,
                       rnn_ref, hl_ref):
    n_t = xr_ref.shape[0]
    n_c = cs_ref.shape[0]
    rows = [cs_ref[i] for i in range(n_c)] + [xr_ref[t] for t in range(n_t)]
    cw = cw_ref[...]
    h = h0_ref[...]
    for t in range(n_t):
        y = cb_ref[...] + rows[t] * cw[0:1]
        for j in range(1, CONV_W):
            y = y + rows[t + j] * cw[j:j + 1]
        a, u = _lru_gates(y, wa_ref, ba_ref[...], wi_ref, bi_ref[...], lam_ref[...])
        h = a * h + u
        rnn_ref[t] = (_gelu_tanh(xg_ref[t]) * h).astype(BF16)
    hl_ref[...] = h


def _rglru_sample(xr_t, xg_t, conv_t, h0, cw, cb, wa_bd, ba, wi_bd, bi, lam):
    n_t, n_seq, rw = xr_t.shape
    return pl.pallas_call(
        _rglru_sample_body,
        out_shape=[
            jax.ShapeDtypeStruct((n_t, n_seq, rw), BF16),
            jax.ShapeDtypeStruct((n_seq, rw), F32),
        ],
        compiler_params=pltpu.CompilerParams(vmem_limit_bytes=VMEM_LIMIT),
        name="rglru_sample",
    )(xr_t, xg_t, conv_t, h0, cw, cb, wa_bd, ba, wi_bd, bi, lam)


def _route(logits):
    g = [logits[i:i + 1] for i in range(N_GROUPS)]
    gmax = functools.reduce(jnp.maximum, g)
    g_idx = jnp.full(gmax.shape, N_GROUPS - 1, jnp.int32)
    for i in range(N_GROUPS - 2, -1, -1):
        g_idx = jnp.where(g[i] == gmax, i, g_idx)
    g_sel = 1.0 / functools.reduce(jnp.add, [jnp.exp(gi - gmax) for gi in g])
    e = []
    for j in range(EXP_PER_GROUP):
        ej = logits[N_GROUPS + j:N_GROUPS + j + 1]
        for grp in range(1, N_GROUPS):
            r = N_GROUPS + grp * EXP_PER_GROUP + j
            ej = jnp.where(g_idx == grp, logits[r:r + 1], ej)
        e.append(ej)
    v1 = functools.reduce(jnp.maximum, e)
    i1 = jnp.full(v1.shape, EXP_PER_GROUP - 1, jnp.int32)
    for j in range(EXP_PER_GROUP - 2, -1, -1):
        i1 = jnp.where(e[j] == v1, j, i1)
    rest = [jnp.where(i1 == j, -jnp.inf, e[j]) for j in range(EXP_PER_GROUP)]
    v2 = functools.reduce(jnp.maximum, rest)
    i2 = jnp.full(v2.shape, EXP_PER_GROUP - 1, jnp.int32)
    for j in range(EXP_PER_GROUP - 2, -1, -1):
        i2 = jnp.where(rest[j] == v2, j, i2)
    z = jnp.exp(v2 - v1)
    w1 = g_sel / (1.0 + z)
    w2 = g_sel * z / (1.0 + z)
    e1 = g_idx * EXP_PER_GROUP + i1
    e2 = g_idx * EXP_PER_GROUP + i2
    rows = [jnp.where(e1 == x, w1, 0.0) + jnp.where(e2 == x, w2, 0.0) for x in range(N_EXPERTS)]
    return jnp.concatenate(rows, axis=0)


def _outproj_body(x_ref, attp_ref, atts_ref, rnnp_ref, rnns_ref, wo_ref, g2_ref, wr_ref, br_ref,
                  h1_ref, hn_ref, gates_ref, *, n_prompt_tiles):
    is_prompt = pl.program_id(0) < n_prompt_tiles
    aw = attp_ref.shape[1]
    att = jnp.where(is_prompt, attp_ref[...], atts_ref[...])
    rnn = jnp.where(is_prompt, rnnp_ref[...], rnns_ref[...])
    mix = jnp.dot(att, wo_ref[0:aw, :], preferred_element_type=F32)
    mix = mix + jnp.dot(rnn, wo_ref[aw:, :], preferred_element_type=F32)
    h1 = x_ref[...] + mix
    h1_ref[...] = h1
    ms = jnp.mean(h1 * h1, axis=-1, keepdims=True)
    hn = h1 * lax.rsqrt(ms + NORM_EPS) * g2_ref[...]
    hn_ref[...] = hn.astype(BF16)
    logits = lax.dot_general(wr_ref[...], hn, (((1,), (1,)), ((), ())),
                             precision=lax.Precision.HIGHEST, preferred_element_type=F32) + br_ref[...]
    gates_ref[...] = _route(logits)


def _outproj(x_all, att_p, att_s, rnn_p, rnn_s, w_out_b, norm2, w_router, b_router):
    nt, d = x_all.shape
    tm = ROW_TILE
    npt = att_p.shape[0] // tm
    nst = att_s.shape[0] // tm
    row = lambda i: (i, 0)
    const = lambda i: (0, 0)
    prow = lambda i: (jnp.minimum(i, npt - 1), 0)
    srow = lambda i: (jnp.clip(i - npt, 0, nst - 1), 0)
    kern = functools.partial(_outproj_body, n_prompt_tiles=npt)
    return pl.pallas_call(
        kern,
        grid=(nt // tm,),
        in_specs=[
            pl.BlockSpec((tm, d), row),
            pl.BlockSpec((tm, att_p.shape[1]), prow),
            pl.BlockSpec((tm, att_s.shape[1]), srow),
            pl.BlockSpec((tm, rnn_p.shape[1]), prow),
            pl.BlockSpec((tm, rnn_s.shape[1]), srow),
            pl.BlockSpec(w_out_b.shape, const),
            pl.BlockSpec(norm2.shape, const),
            pl.BlockSpec(w_router.shape, const),
            pl.BlockSpec(b_router.shape, const),
        ],
        out_specs=[
            pl.BlockSpec((tm, d), row),
            pl.BlockSpec((tm, d), row),
            pl.BlockSpec((N_EXPERTS, tm), lambda i: (0, i)),
        ],
        out_shape=[
            jax.ShapeDtypeStruct((nt, d), F32),
            jax.ShapeDtypeStruct((nt, d), BF16),
            jax.ShapeDtypeStruct((N_EXPERTS, nt), F32),
        ],
        compiler_params=_params(("parallel",)),
        name="outproj_router",
    )(x_all, att_p, att_s, rnn_p, rnn_s, w_out_b, norm2, w_router, b_router)


def _moe_body(hn_ref, h1_ref, gates_ref, wg_ref, wu_ref, wd_ref, nf_ref, y_ref, acc_sc):
    e = pl.program_id(1)

    @pl.when(e == 0)
    def _():
        acc_sc[...] = jnp.zeros(acc_sc.shape, F32)

    x = hn_ref[...]
    hg = jnp.dot(x, wg_ref[...], preferred_element_type=F32)
    hu = jnp.dot(x, wu_ref[...], preferred_element_type=F32)
    he = (hg * _sigmoid(hg) * hu).astype(BF16)
    out = jnp.dot(he, wd_ref[...], preferred_element_type=F32)
    gates = gates_ref[...]
    lane = lax.broadcasted_iota(jnp.int32, gates.shape, 1)
    gcol = jnp.sum(jnp.where(lane == e, gates, 0.0), axis=1, keepdims=True)
    acc_sc[...] = acc_sc[...] + gcol * out

    @pl.when(e == pl.num_programs(1) - 1)
    def _():
        h2 = h1_ref[...] + acc_sc[...]
        ms = jnp.mean(h2 * h2, axis=-1, keepdims=True)
        y_ref[...] = h2 * lax.rsqrt(ms + NORM_EPS) * nf_ref[...]


def _moe(hn2, h1, gates, wg_b, wu_b, wd_b, norm_f):
    nt, d = h1.shape
    de = wg_b.shape[2]
    tm = 512 if nt % 512 == 0 else ROW_TILE
    row = lambda i, e: (i, 0)
    return pl.pallas_call(
        _moe_body,
        grid=(nt // tm, N_EXPERTS),
        in_specs=[
            pl.BlockSpec((tm, d), row),
            pl.BlockSpec((tm, d), row),
            pl.BlockSpec((tm, N_EXPERTS), row),
            pl.BlockSpec((None, d, de), lambda i, e: (e, 0, 0)),
            pl.BlockSpec((None, d, de), lambda i, e: (e, 0, 0)),
            pl.BlockSpec((None, de, d), lambda i, e: (e, 0, 0)),
            pl.BlockSpec((1, d), lambda i, e: (0, 0)),
        ],
        out_specs=pl.BlockSpec((tm, d), row),
        out_shape=jax.ShapeDtypeStruct((nt, d), F32),
        scratch_shapes=[pltpu.VMEM((tm, d), F32)],
        compiler_params=_params(("parallel", "arbitrary")),
        name="moe_dense",
    )(hn2, h1, gates, wg_b, wu_b, wd_b, norm_f)


def _rope_tables(pos):
    half = ROPE_DIM // 2
    inv = jnp.power(ROPE_THETA, -2.0 * jnp.arange(half, dtype=F32) / ROPE_DIM)
    ang = pos[:, None] * inv[None, :]
    cos, sin = jnp.cos(ang), jnp.sin(ang)
    n = pos.shape[0]
    pad = jnp.zeros((n, HEAD_QK - ROPE_DIM), F32)
    ct = jnp.concatenate([cos, cos, pad + 1.0], axis=1)
    s1 = jnp.concatenate([-sin, jnp.zeros_like(sin), pad], axis=1)
    s2 = jnp.concatenate([jnp.zeros_like(sin), sin, pad], axis=1)
    rep = LANES // HEAD_QK
    return jnp.stack([jnp.tile(t, (1, rep)) for t in (ct, s1, s2)])


def _block_diag_halves(w):
    nb, bs, _ = w.shape
    hb = nb // 2
    eye = jnp.eye(hb, dtype=w.dtype)
    halves = [jnp.einsum('nkj,nm->nkmj', w[i * hb:(i + 1) * hb], eye).reshape(hb * bs, hb * bs) for i in range(2)]
    return jnp.stack(halves).astype(BF16)


def kernel(x_prompt, x_sample, cache_k, cache_v, state_conv, state_h, page_table, meta_tokens, norm1, w_in,
           lambda_q1, lambda_k1, lambda_q2, lambda_k2, subln, conv_w, conv_b, rg_w_a, rg_b_a, rg_w_i, rg_b_i,
           lru_lambda, w_out, norm2, w_grp, b_grp, w_rt, b_rt, w_gate, w_up, w_down, norm_f):
    assert w_in.shape[0] == 1, "single-layer trunk only"
    n_batch, seq, d = x_prompt.shape
    n_seq, n_t, _ = x_sample.shape
    assert seq % ROW_TILE == 0 and n_t >= CONV_W - 1
    front = ROW_TILE - N_META
    lp = ROW_TILE + seq
    n_pages, page = page_table.shape[1], cache_k.shape[2]
    past_len = n_pages * page
    ns = n_seq * n_t
    ns_pad = -(-ns // ROW_TILE) * ROW_TILE
    np_rows = n_batch * lp
    rw = lru_lambda.shape[1]

    head = jnp.concatenate([jnp.zeros((front, d), F32), meta_tokens.astype(F32)], axis=0)
    xp = jnp.concatenate([jnp.broadcast_to(head[None], (n_batch, ROW_TILE, d)), x_prompt], axis=1)
    x_all = jnp.concatenate([xp.reshape(np_rows, d), x_sample.reshape(ns, d),
                             jnp.zeros((ns_pad - ns, d), F32)], axis=0)
    pos_p = jnp.maximum(jnp.arange(lp) - front, 0)
    pos_s = past_len + jnp.arange(n_t)
    pos = jnp.concatenate([jnp.tile(pos_p, n_batch), jnp.tile(pos_s, n_seq),
                           jnp.zeros((ns_pad - ns,), pos_p.dtype)]).astype(F32)
    cs = _rope_tables(pos)

    lamv = jnp.stack([lambda_q1[0], lambda_k1[0], lambda_q2[0], lambda_k2[0]]).astype(F32)
    gain = subln.astype(F32)

    q_all, k_all, kb_all, v_all, vb_all, xr_all, xg_all = _inproj(x_all, norm1, w_in[0].astype(BF16), cs)

    att_p = _prompt_attention(q_all, kb_all, vb_all, lamv, gain, n_batch, lp, front)
    wa_bd, wi_bd = _block_diag_halves(rg_w_a[0]), _block_diag_halves(rg_w_i[0])
    rnn_w = (conv_w[0], conv_b, wa_bd, rg_b_a, wi_bd, rg_b_i, lru_lambda)
    rnn_p, hl_p = _rglru_prompt(xr_all, xg_all, *rnn_w, n_batch, lp, front)

    sl = slice(np_rows, np_rows + ns)
    q_s = q_all[sl].reshape(n_seq, n_t, ATT_HEADS * 2, HEAD_QK)
    eye = jnp.eye(ATT_HEADS * 2, dtype=BF16)
    qbd = jnp.einsum('bthd,hg->bhtgd', q_s, eye).reshape(n_seq, ATT_HEADS * 2 * n_t, Q_WIDTH)
    att_s = _sample_attention(qbd, k_all[sl].reshape(n_seq, n_t, Q_WIDTH), v_all[sl].reshape(n_seq, n_t, ATT_WIDTH),
                              cache_k[0].reshape(-1, page, Q_WIDTH), cache_v[0].reshape(-1, page, ATT_WIDTH),
                              page_table, lamv, gain)
    att_s = jnp.concatenate([att_s.reshape(ns, ATT_WIDTH).astype(BF16),
                             jnp.zeros((ns_pad - ns, ATT_WIDTH), BF16)], axis=0)
    to_tmajor = lambda a: jnp.swapaxes(a[sl].reshape(n_seq, n_t, rw), 0, 1)
    rnn_s, hl_s = _rglru_sample(to_tmajor(xr_all), to_tmajor(xg_all), jnp.swapaxes(state_conv[0], 0, 1),
                                state_h[0].astype(F32), *rnn_w)
    rnn_s = jnp.concatenate([jnp.swapaxes(rnn_s, 0, 1).reshape(ns, rw), jnp.zeros((ns_pad - ns, rw), BF16)], axis=0)

    w_router = jnp.concatenate([w_grp[0].T, w_rt[0].T,
                                jnp.zeros((32 - N_GROUPS - N_EXPERTS, d), F32)], axis=0).astype(F32)
    b_router = jnp.concatenate([b_grp[0], b_rt[0], jnp.zeros((32 - N_GROUPS - N_EXPERTS,), F32)])[:, None]
    h1, hn2, gates_t = _outproj(x_all, att_p, att_s, rnn_p, rnn_s, w_out[0].astype(BF16), norm2,
                                w_router, b_router.astype(F32))
    y_all = _moe(hn2, h1, gates_t.T, w_gate[0].astype(BF16), w_up[0].astype(BF16), w_down[0].astype(BF16),
                 norm_f[None, :])

    def prompt_rows(a):
        return a[:np_rows].reshape(n_batch, lp, -1)

    y_prompt = prompt_rows(y_all)[:, ROW_TILE:]
    y_sample = y_all[sl].reshape(n_seq, n_t, d)
    k_prompt = prompt_rows(k_all)[:, front:].reshape(1, n_batch, N_META + seq, ATT_HEADS, 2, HEAD_QK)
    v_prompt = prompt_rows(v_all)[:, front:].reshape(1, n_batch, N_META + seq, ATT_HEADS, HEAD_V)
    k_sample = k_all[sl].reshape(1, n_seq, n_t, ATT_HEADS, 2, HEAD_QK)
    v_sample = v_all[sl].reshape(1, n_seq, n_t, ATT_HEADS, HEAD_V)
    conv_prompt = prompt_rows(xr_all)[:, lp - (CONV_W - 1):][None]
    conv_sample = xr_all[sl].reshape(n_seq, n_t, rw)[:, n_t - (CONV_W - 1):][None]
    h_prompt = hl_p.reshape(n_batch, 8, rw)[:, 7][None]
    h_sample = hl_s[None]
    return (y_prompt, y_sample, k_prompt, v_prompt, k_sample, v_sample, conv_prompt, conv_sample,
            h_prompt, h_sample)
```

```python
import functools
import math

import jax
import jax.numpy as jnp
from jax import lax
from jax.experimental import pallas as pl
from jax.experimental.pallas import tpu as pltpu

F32 = jnp.float32
BF16 = jnp.bfloat16

N_META = 16
ATT_HEADS = 4
HEAD_QK = 64
HEAD_V = 2 * HEAD_QK
ATT_WIDTH = ATT_HEADS * HEAD_V
Q_WIDTH = ATT_HEADS * 2 * HEAD_QK
ROPE_DIM = HEAD_QK // 4
ROPE_THETA = 500000.0
RNN_BLOCKS = 8
CONV_W = 4
LRU_C = 8.0
N_GROUPS = 4
EXP_PER_GROUP = 4
N_EXPERTS = N_GROUPS * EXP_PER_GROUP
NORM_EPS = 1e-6
NEG_INF = -1e30
LAM_INIT = 0.8 - 0.6 * math.exp(-0.3 * 0)

LANES = 128
ROW_TILE = 256
VMEM_LIMIT = 56 * 1024 * 1024


def _params(sem, vmem=VMEM_LIMIT):
    return pltpu.CompilerParams(dimension_semantics=sem, vmem_limit_bytes=vmem)


def _sigmoid(x):
    return 1.0 / (1.0 + jnp.exp(-x))


def _softplus(x):
    return jnp.maximum(x, 0.0) + jnp.log1p(jnp.exp(-jnp.abs(x)))


def _gelu_tanh(x):
    return 0.5 * x * (1.0 + jnp.tanh(math.sqrt(2.0 / math.pi) * (x + 0.044715 * (x * x * x))))


def _lam_scalar(lamv):
    s1 = jnp.sum(lamv[0:1] * lamv[1:2], axis=-1, keepdims=True)
    s2 = jnp.sum(lamv[2:3] * lamv[3:4], axis=-1, keepdims=True)
    return jnp.exp(s1) - jnp.exp(s2) + LAM_INIT


def _inproj_body(x_ref, g_ref, w_ref, cs_ref, q_ref, k_ref, kb_ref, v_ref, vb_ref, xr_ref, xg_ref):
    x = x_ref[...]
    ms = jnp.mean(x * x, axis=-1, keepdims=True)
    hn = (x * lax.rsqrt(ms + NORM_EPS) * g_ref[...]).astype(BF16)
    proj = jnp.dot(hn, w_ref[...], preferred_element_type=F32)
    ct, s1, s2 = cs_ref[0], cs_ref[1], cs_ref[2]

    def rope(xc):
        return xc * ct + pltpu.roll(xc, LANES - ROPE_DIM // 2, 1) * s1 + pltpu.roll(xc, ROPE_DIM // 2, 1) * s2

    for c in range(Q_WIDTH // LANES):
        sl = slice(c * LANES, (c + 1) * LANES)
        qc = rope(proj[:, c * LANES:(c + 1) * LANES])
        q_ref[:, sl] = (qc * (HEAD_QK ** -0.5)).astype(BF16)
        kc = rope(proj[:, Q_WIDTH + c * LANES:Q_WIDTH + (c + 1) * LANES])
        k_ref[:, sl] = kc
        kb_ref[:, sl] = kc.astype(BF16)
    v = proj[:, 2 * Q_WIDTH:2 * Q_WIDTH + ATT_WIDTH]
    v_ref[...] = v
    vb_ref[...] = v.astype(BF16)
    r0 = 2 * Q_WIDTH + ATT_WIDTH
    rw = xr_ref.shape[1]
    xr_ref[...] = proj[:, r0:r0 + rw]
    xg_ref[...] = proj[:, r0 + rw:r0 + 2 * rw]


def _inproj(x_all, norm1, w_in_b, cs):
    nt, d = x_all.shape
    rw = (w_in_b.shape[1] - 2 * Q_WIDTH - ATT_WIDTH) // 2
    tm = ROW_TILE
    row = lambda i: (i, 0)
    const = lambda i: (0, 0)
    outs = [
        jax.ShapeDtypeStruct((nt, Q_WIDTH), BF16),
        jax.ShapeDtypeStruct((nt, Q_WIDTH), F32),
        jax.ShapeDtypeStruct((nt, Q_WIDTH), BF16),
        jax.ShapeDtypeStruct((nt, ATT_WIDTH), F32),
        jax.ShapeDtypeStruct((nt, ATT_WIDTH), BF16),
        jax.ShapeDtypeStruct((nt, rw), F32),
        jax.ShapeDtypeStruct((nt, rw), F32),
    ]
    return pl.pallas_call(
        _inproj_body,
        grid=(nt // tm,),
        in_specs=[
            pl.BlockSpec((tm, d), row),
            pl.BlockSpec((1, d), const),
            pl.BlockSpec(w_in_b.shape, const),
            pl.BlockSpec((3, tm, LANES), lambda i: (0, i, 0)),
        ],
        out_specs=[pl.BlockSpec((tm, o.shape[1]), row) for o in outs],
        out_shape=outs,
        compiler_params=_params(("parallel",)),
        name="inproj",
    )(x_all, norm1, w_in_b, cs)


def _flash_body(q_ref, k_ref, v_ref, lamv_ref, gain_ref, o_ref, q2_sc, m_sc, acc_sc, *, tq, front):
    qi = pl.program_id(1)
    lane = lax.broadcasted_iota(jnp.int32, (tq, LANES), 1)
    for h in range(ATT_HEADS):
        q = q_ref[:, h * LANES:(h + 1) * LANES]
        zero = jnp.zeros_like(q)
        q2_sc[h] = jnp.concatenate([jnp.where(lane < HEAD_QK, q, zero), jnp.where(lane >= HEAD_QK, q, zero)], axis=0)
    m_sc[...] = jnp.full(m_sc.shape, NEG_INF, F32)
    acc_sc[...] = jnp.zeros(acc_sc.shape, F32)
    ones = jnp.ones((tq, LANES), BF16)
    n_ch = tq // LANES

    def step(ki, masked):
        start = pl.multiple_of(ki * tq, tq)
        for h in range(ATT_HEADS):
            k = k_ref[pl.ds(start, tq), h * LANES:(h + 1) * LANES]
            v = v_ref[pl.ds(start, tq), h * LANES:(h + 1) * LANES]
            s = lax.dot_general(q2_sc[h], k, (((1,), (1,)), ((), ())), preferred_element_type=F32)
            if masked:
                row = lax.broadcasted_iota(jnp.int32, s.shape, 0)
                col = lax.broadcasted_iota(jnp.int32, s.shape, 1)
                qpos = qi * tq + jnp.where(row >= tq, row - tq, row)
                kpos = ki * tq + col
                s = jnp.where(kpos <= qpos, jnp.where(kpos >= front, s, NEG_INF), NEG_INF)
            chunks = [s[:, c * LANES:(c + 1) * LANES] for c in range(n_ch)]
            m_prev = m_sc[h]
            m_cur = jnp.max(functools.reduce(jnp.maximum, chunks), axis=1, keepdims=True)
            m_next = jnp.maximum(m_prev, m_cur)
            p = jnp.concatenate([jnp.exp(c - m_next) for c in chunks], axis=1).astype(BF16)
            alpha = jnp.exp(m_prev - m_next)
            pv = jnp.dot(p, jnp.concatenate([v, ones], axis=1), preferred_element_type=F32)
            acc_sc[h] = jnp.concatenate([alpha, alpha], axis=1) * acc_sc[h] + pv
            m_sc[h] = m_next

    step(0, True)

    def body(ki, carry):
        step(ki, False)
        return carry

    lax.fori_loop(1, qi, body, 0)

    @pl.when(qi > 0)
    def _():
        step(qi, True)

    lam = _lam_scalar(lamv_ref[...])
    for h in range(ATT_HEADS):
        acc = acc_sc[h]
        o = acc[:, :HEAD_V] / acc[:, HEAD_V:]
        d = o[:tq] - lam * o[tq:]
        ms = jnp.mean(d * d, axis=-1, keepdims=True)
        o_ref[:, h * HEAD_V:(h + 1) * HEAD_V] = (
            d * lax.rsqrt(ms + NORM_EPS) * gain_ref[...] * (1.0 - LAM_INIT)).astype(BF16)


def _prompt_attention(q_all, kb_all, vb_all, lamv, gain, n_batch, lp, front):
    tq = ROW_TILE
    nq = lp // tq
    kern = functools.partial(_flash_body, tq=tq, front=front)
    return pl.pallas_call(
        kern,
        grid=(n_batch, nq),
        in_specs=[
            pl.BlockSpec((tq, Q_WIDTH), lambda b, i: (b * nq + i, 0)),
            pl.BlockSpec((lp, Q_WIDTH), lambda b, i: (b, 0)),
            pl.BlockSpec((lp, ATT_WIDTH), lambda b, i: (b, 0)),
            pl.BlockSpec(lamv.shape, lambda b, i: (0, 0)),
            pl.BlockSpec(gain.shape, lambda b, i: (0, 0)),
        ],
        out_specs=pl.BlockSpec((tq, ATT_WIDTH), lambda b, i: (b * nq + i, 0)),
        out_shape=jax.ShapeDtypeStruct((n_batch * lp, ATT_WIDTH), BF16),
        scratch_shapes=[
            pltpu.VMEM((ATT_HEADS, 2 * tq, LANES), BF16),
            pltpu.VMEM((ATT_HEADS, 2 * tq, LANES), F32),
            pltpu.VMEM((ATT_HEADS, 2 * tq, 2 * HEAD_V), F32),
        ],
        compiler_params=_params(("parallel", "arbitrary")),
        name="prompt_attn",
    )(q_all, kb_all, vb_all, lamv, gain)


def _paged_body(pt_ref, qbd_ref, kn_ref, vn_ref, lamv_ref, gain_ref, exp_ref, *rest, n_pg, n_chunks, n_t):
    del pt_ref
    k_refs = rest[:n_pg]
    v_refs = rest[n_pg:2 * n_pg]
    o_ref = rest[2 * n_pg]
    m_sc, l_sc, acc_sc = rest[2 * n_pg + 1:]
    j = pl.program_id(1)
    n_rows = qbd_ref.shape[0]

    @pl.when(j == 0)
    def _():
        m_sc[...] = jnp.full(m_sc.shape, NEG_INF, F32)
        l_sc[...] = jnp.zeros(l_sc.shape, F32)
        acc_sc[...] = jnp.zeros(acc_sc.shape, F32)

    qbd = qbd_ref[...]
    chunks = [jnp.dot(qbd, k_refs[i][...].astype(BF16), preferred_element_type=F32) for i in range(n_pg)]
    m_prev = m_sc[...]
    m_cur = jnp.max(functools.reduce(jnp.maximum, chunks), axis=1, keepdims=True)
    m_next = jnp.maximum(m_prev, m_cur)
    p = [jnp.exp(c - m_next) for c in chunks]
    alpha = jnp.exp(m_prev - m_next)
    l_sc[...] = alpha * l_sc[...] + jnp.sum(functools.reduce(jnp.add, p), axis=1, keepdims=True)
    expand = exp_ref[...]
    pw = expand.shape[1]
    own_head = (lax.broadcasted_iota(jnp.int32, (n_rows, pw), 1) % ATT_HEADS
                == lax.broadcasted_iota(jnp.int32, (n_rows, pw), 0) // (2 * n_t))
    pv = None
    for i in range(n_pg):
        pe = jnp.dot(p[i].astype(BF16), expand, preferred_element_type=F32)
        pe = jnp.where(own_head, pe, 0.0).astype(BF16)
        d = jnp.dot(pe, v_refs[i][...].astype(BF16), preferred_element_type=F32)
        pv = d if pv is None else pv + d
    acc_sc[...] = acc_sc[...] * alpha + pv
    m_sc[...] = m_next

    @pl.when(j == n_chunks - 1)
    def _():
        qf = qbd.astype(F32)
        kn = kn_ref[...].astype(BF16).astype(F32)
        vn = vn_ref[...].astype(BF16).astype(F32)
        row_t = lax.broadcasted_iota(jnp.int32, (n_rows, 1), 0) % n_t
        s_new = []
        for t2 in range(n_t):
            st = jnp.sum(qf * kn[t2:t2 + 1, :], axis=1, keepdims=True)
            s_new.append(jnp.where(row_t >= t2, st, NEG_INF))
        m_old = m_sc[...]
        m_fin = m_old
        for st in s_new:
            m_fin = jnp.maximum(m_fin, st)
        a_fin = jnp.exp(m_old - m_fin)
        l_fin = a_fin * l_sc[...]
        acc = acc_sc[...] * a_fin
        for t2 in range(n_t):
            pt = jnp.exp(s_new[t2] - m_fin)
            l_fin = l_fin + pt
            v_rows = jnp.concatenate(
                [jnp.broadcast_to(vn[t2:t2 + 1, h * HEAD_V:(h + 1) * HEAD_V], (2 * n_t, HEAD_V))
                 for h in range(ATT_HEADS)], axis=0)
            acc = acc + pt * v_rows
        o = acc / l_fin
        lam = _lam_scalar(lamv_ref[...])
        for h in range(ATT_HEADS):
            r0 = h * 2 * n_t
            d = o[r0:r0 + n_t] - lam * o[r0 + n_t:r0 + 2 * n_t]
            ms = jnp.mean(d * d, axis=-1, keepdims=True)
            o_ref[:, h * HEAD_V:(h + 1) * HEAD_V] = d * lax.rsqrt(ms + NORM_EPS) * gain_ref[...] * (1.0 - LAM_INIT)


def _sample_attention(qbd, k_new, v_new, cache_kt, cache_vm, page_table, lamv, gain):
    n_seq, n_rows, kw = qbd.shape
    n_t = k_new.shape[1]
    n_pages = page_table.shape[1]
    page = cache_kt.shape[2]
    n_pg = math.gcd(n_pages, 16)
    n_chunks = n_pages // n_pg
    pt_flat = page_table.reshape(-1)
    expand = jnp.repeat(jnp.eye(page, dtype=BF16), ATT_HEADS, axis=1)

    def page_map(i):
        return lambda b, j, pt: (pt[b * n_pages + j * n_pg + i], 0, 0)

    seq3 = lambda b, j, pt: (b, 0, 0)
    const2 = lambda b, j, pt: (0, 0)
    kern = functools.partial(_paged_body, n_pg=n_pg, n_chunks=n_chunks, n_t=n_t)
    grid_spec = pltpu.PrefetchScalarGridSpec(
        num_scalar_prefetch=1,
        grid=(n_seq, n_chunks),
        in_specs=[
            pl.BlockSpec((None, n_rows, kw), seq3),
            pl.BlockSpec((None, n_t, kw), seq3),
            pl.BlockSpec((None, n_t, kw), seq3),
            pl.BlockSpec(lamv.shape, const2),
            pl.BlockSpec(gain.shape, const2),
            pl.BlockSpec(expand.shape, const2),
        ] + [pl.BlockSpec((None,) + cache_kt.shape[1:], page_map(i)) for i in range(n_pg)]
          + [pl.BlockSpec((None,) + cache_vm.shape[1:], page_map(i)) for i in range(n_pg)],
        out_specs=pl.BlockSpec((None, n_t, ATT_WIDTH), seq3),
        scratch_shapes=[
            pltpu.VMEM((n_rows, LANES), F32),
            pltpu.VMEM((n_rows, LANES), F32),
            pltpu.VMEM((n_rows, HEAD_V), F32),
        ],
    )
    return pl.pallas_call(
        kern,
        grid_spec=grid_spec,
        out_shape=jax.ShapeDtypeStruct((n_seq, n_t, ATT_WIDTH), F32),
        compiler_params=_params(("parallel", "arbitrary")),
        name="sample_attn",
    )(pt_flat, qbd, k_new, v_new, lamv, gain, expand, *([cache_kt] * n_pg), *([cache_vm] * n_pg))


def _lru_gates(xc, wa_ref, ba, wi_ref, bi, lam_row):
    half = xc.shape[1] // 2
    xb = xc.astype(BF16)

    def blockdiag(w_ref):
        return jnp.concatenate([
            jnp.dot(xb[:, :half], w_ref[0], preferred_element_type=F32),
            jnp.dot(xb[:, half:], w_ref[1], preferred_element_type=F32)], axis=1)

    gate_r = _sigmoid(blockdiag(wa_ref) + ba)
    gate_i = _sigmoid(blockdiag(wi_ref) + bi)
    log_a = -LRU_C * gate_r * _softplus(-lam_row)
    a = jnp.exp(log_a)
    th = jnp.tanh(log_a)
    u = jnp.sqrt(-2.0 * th / (1.0 - th)) * (gate_i * xc)
    return a, u


def _rglru_prompt_body(xr_ref, xg_ref, cw_ref, cb_ref, wa_ref, ba_ref, wi_ref, bi_ref, lam_ref,
                       rnn_ref, hl_ref, prev_sc, h_sc, *, tc, front):
    c = pl.program_id(1)

    @pl.when(c == 0)
    def _():
        prev_sc[...] = jnp.zeros(prev_sc.shape, F32)
        h_sc[...] = jnp.zeros(h_sc.shape, F32)

    x = xr_ref[...]
    xfull = jnp.concatenate([prev_sc[...], x], axis=0)
    cw = cw_ref[...]
    y = cb_ref[...] + xfull[5:5 + tc] * cw[0:1]
    y = y + xfull[6:6 + tc] * cw[1:2]
    y = y + xfull[7:7 + tc] * cw[2:3]
    y = y + x * cw[3:4]
    prev_sc[...] = x[tc - 8:tc]
    a, u = _lru_gates(y, wa_ref, ba_ref[...], wi_ref, bi_ref[...], lam_ref[...])
    row = lax.broadcasted_iota(jnp.int32, (tc, 1), 0)
    u = jnp.where(c * tc + row >= front, u, 0.0)
    d = 1
    while d < tc:
        keep = row >= d
        u = jnp.where(keep, a * pltpu.roll(u, d, 0) + u, u)
        a = jnp.where(keep, a * pltpu.roll(a, d, 0), a)
        d *= 2
    h = a * h_sc[0:1, :] + u
    h_sc[...] = h[tc - 8:tc]
    h_sc[0:1, :] = h[tc - 1:tc]
    hl_ref[...] = h[tc - 8:tc]
    rnn_ref[...] = (_gelu_tanh(xg_ref[...]) * h).astype(BF16)


def _rglru_prompt(xr_all, xg_all, cw, cb, wa_bd, ba, wi_bd, bi, lam, n_batch, lp, front):
    tc = ROW_TILE
    nc = lp // tc
    rw = xr_all.shape[1]
    kern = functools.partial(_rglru_prompt_body, tc=tc, front=front)
    row = lambda b, c: (b * nc + c, 0)
    const2 = lambda b, c: (0, 0)
    const3 = lambda b, c: (0, 0, 0)
    return pl.pallas_call(
        kern,
        grid=(n_batch, nc),
        in_specs=[
            pl.BlockSpec((tc, rw), row),
            pl.BlockSpec((tc, rw), row),
            pl.BlockSpec(cw.shape, const2),
            pl.BlockSpec(cb.shape, const2),
            pl.BlockSpec(wa_bd.shape, const3),
            pl.BlockSpec(ba.shape, const2),
            pl.BlockSpec(wi_bd.shape, const3),
            pl.BlockSpec(bi.shape, const2),
            pl.BlockSpec(lam.shape, const2),
        ],
        out_specs=[
            pl.BlockSpec((tc, rw), row),
            pl.BlockSpec((8, rw), lambda b, c: (b, 0)),
        ],
        out_shape=[
            jax.ShapeDtypeStruct((n_batch * lp, rw), BF16),
            jax.ShapeDtypeStruct((n_batch * 8, rw), F32),
        ],
        scratch_shapes=[pltpu.VMEM((8, rw), F32), pltpu.VMEM((8, rw), F32)],
        compiler_params=_params(("parallel", "arbitrary")),
        name="rglru_prompt",
    )(xr_all, xg_all, cw, cb, wa_bd, ba, wi_bd, bi, lam)


def _rglru_sample_body(xr_ref, xg_ref, cs_ref, h0_ref, cw_ref, cb_ref, wa_ref, ba_ref, wi_ref, bi_ref, lam_ref,
                       rnn_ref, hl_ref):
    n_t = xr_ref.shape[0]
    n_c = cs_ref.shape[0]
    rows = [cs_ref[i] for i in range(n_c)] + [xr_ref[t] for t in range(n_t)]
    cw = cw_ref[...]
    h = h0_ref[...]
    for t in range(n_t):
        y = cb_ref[...] + rows[t] * cw[0:1]
        for j in range(1, CONV_W):
            y = y + rows[t + j] * cw[j:j + 1]
        a, u = _lru_gates(y, wa_ref, ba_ref[...], wi_ref, bi_ref[...], lam_ref[...])
        h = a * h + u
        rnn_ref[t] = (_gelu_tanh(xg_ref[t]) * h).astype(BF16)
    hl_ref[...] = h


def _rglru_sample(xr_t, xg_t, conv_t, h0, cw, cb, wa_bd, ba, wi_bd, bi, lam):
    n_t, n_seq, rw = xr_t.shape
    return pl.pallas_call(
        _rglru_sample_body,
        out_shape=[
            jax.ShapeDtypeStruct((n_t, n_seq, rw), BF16),
            jax.ShapeDtypeStruct((n_seq, rw), F32),
        ],
        compiler_params=pltpu.CompilerParams(vmem_limit_bytes=VMEM_LIMIT),
        name="rglru_sample",
    )(xr_t, xg_t, conv_t, h0, cw, cb, wa_bd, ba, wi_bd, bi, lam)


def _route(logits):
    g = [logits[i:i + 1] for i in range(N_GROUPS)]
    gmax = functools.reduce(jnp.maximum, g)
    g_idx = jnp.full(gmax.shape, N_GROUPS - 1, jnp.int32)
    for i in range(N_GROUPS - 2, -1, -1):
        g_idx = jnp.where(g[i] == gmax, i, g_idx)
    g_sel = 1.0 / functools.reduce(jnp.add, [jnp.exp(gi - gmax) for gi in g])
    e = []
    for j in range(EXP_PER_GROUP):
        ej = logits[N_GROUPS + j:N_GROUPS + j + 1]
        for grp in range(1, N_GROUPS):
            r = N_GROUPS + grp * EXP_PER_GROUP + j
            ej = jnp.where(g_idx == grp, logits[r:r + 1], ej)
        e.append(ej)
    v1 = functools.reduce(jnp.maximum, e)
    i1 = jnp.full(v1.shape, EXP_PER_GROUP - 1, jnp.int32)
    for j in range(EXP_PER_GROUP - 2, -1, -1):
        i1 = jnp.where(e[j] == v1, j, i1)
    rest = [jnp.where(i1 == j, -jnp.inf, e[j]) for j in range(EXP_PER_GROUP)]
    v2 = functools.reduce(jnp.maximum, rest)
    i2 = jnp.full(v2.shape, EXP_PER_GROUP - 1, jnp.int32)
    for j in range(EXP_PER_GROUP - 2, -1, -1):
        i2 = jnp.where(rest[j] == v2, j, i2)
    z = jnp.exp(v2 - v1)
    w1 = g_sel / (1.0 + z)
    w2 = g_sel * z / (1.0 + z)
    e1 = g_idx * EXP_PER_GROUP + i1
    e2 = g_idx * EXP_PER_GROUP + i2
    rows = [jnp.where(e1 == x, w1, 0.0) + jnp.where(e2 == x, w2, 0.0) for x in range(N_EXPERTS)]
    return jnp.concatenate(rows, axis=0)


def _outproj_body(x_ref, attp_ref, atts_ref, rnnp_ref, rnns_ref, wo_ref, g2_ref, wr_ref, br_ref,
                  h1_ref, hn_ref, gates_ref, *, n_prompt_tiles):
    is_prompt = pl.program_id(0) < n_prompt_tiles
    aw = attp_ref.shape[1]
    att = jnp.where(is_prompt, attp_ref[...], atts_ref[...])
    rnn = jnp.where(is_prompt, rnnp_ref[...], rnns_ref[...])
    mix = jnp.dot(att, wo_ref[0:aw, :], preferred_element_type=F32)
    mix = mix + jnp.dot(rnn, wo_ref[aw:, :], preferred_element_type=F32)
    h1 = x_ref[...] + mix
    h1_ref[...] = h1
    ms = jnp.mean(h1 * h1, axis=-1, keepdims=True)
    hn = h1 * lax.rsqrt(ms + NORM_EPS) * g2_ref[...]
    hn_ref[...] = hn.astype(BF16)
    logits = lax.dot_general(wr_ref[...], hn, (((1,), (1,)), ((), ())),
                             precision=lax.Precision.HIGHEST, preferred_element_type=F32) + br_ref[...]
    gates_ref[...] = _route(logits)


def _outproj(x_all, att_p, att_s, rnn_p, rnn_s, w_out_b, norm2, w_router, b_router):
    nt, d = x_all.shape
    tm = ROW_TILE
    npt = att_p.shape[0] // tm
    nst = att_s.shape[0] // tm
    row = lambda i: (i, 0)
    const = lambda i: (0, 0)
    prow = lambda i: (jnp.minimum(i, npt - 1), 0)
    srow = lambda i: (jnp.clip(i - npt, 0, nst - 1), 0)
    kern = functools.partial(_outproj_body, n_prompt_tiles=npt)
    return pl.pallas_call(
        kern,
        grid=(nt // tm,),
        in_specs=[
            pl.BlockSpec((tm, d), row),
            pl.BlockSpec((tm, att_p.shape[1]), prow),
            pl.BlockSpec((tm, att_s.shape[1]), srow),
            pl.BlockSpec((tm, rnn_p.shape[1]), prow),
            pl.BlockSpec((tm, rnn_s.shape[1]), srow),
            pl.BlockSpec(w_out_b.shape, const),
            pl.BlockSpec(norm2.shape, const),
            pl.BlockSpec(w_router.shape, const),
            pl.BlockSpec(b_router.shape, const),
        ],
        out_specs=[
            pl.BlockSpec((tm, d), row),
            pl.BlockSpec((tm, d), row),
            pl.BlockSpec((N_EXPERTS, tm), lambda i: (0, i)),
        ],
        out_shape=[
            jax.ShapeDtypeStruct((nt, d), F32),
            jax.ShapeDtypeStruct((nt, d), BF16),
            jax.ShapeDtypeStruct((N_EXPERTS, nt), F32),
        ],
        compiler_params=_params(("parallel",)),
        name="outproj_router",
    )(x_all, att_p, att_s, rnn_p, rnn_s, w_out_b, norm2, w_router, b_router)


def _moe_body(hn_ref, h1_ref, gates_ref, wg_ref, wu_ref, wd_ref, nf_ref, y_ref, acc_sc):
    e = pl.program_id(1)

    @pl.when(e == 0)
    def _():
        acc_sc[...] = jnp.zeros(acc_sc.shape, F32)

    x = hn_ref[...]
    hg = jnp.dot(x, wg_ref[...], preferred_element_type=F32)
    hu = jnp.dot(x, wu_ref[...], preferred_element_type=F32)
    he = (hg * _sigmoid(hg) * hu).astype(BF16)
    out = jnp.dot(he, wd_ref[...], preferred_element_type=F32)
    gates = gates_ref[...]
    lane = lax.broadcasted_iota(jnp.int32, gates.shape, 1)
    gcol = jnp.sum(jnp.where(lane == e, gates, 0.0), axis=1, keepdims=True)
    acc_sc[...] = acc_sc[...] + gcol * out

    @pl.when(e == pl.num_programs(1) - 1)
    def _():
        h2 = h1_ref[...] + acc_sc[...]
        ms = jnp.mean(h2 * h2, axis=-1, keepdims=True)
        y_ref[...] = h2 * lax.rsqrt(ms + NORM_EPS) * nf_ref[...]


def _moe(hn2, h1, gates, wg_b, wu_b, wd_b, norm_f):
    nt, d = h1.shape
    de = wg_b.shape[2]
    tm = 512 if nt % 512 == 0 else ROW_TILE
    row = lambda i, e: (i, 0)
    return pl.pallas_call(
        _moe_body,
        grid=(nt // tm, N_EXPERTS),
        in_specs=[
            pl.BlockSpec((tm, d), row),
            pl.BlockSpec((tm, d), row),
            pl.BlockSpec((tm, N_EXPERTS), row),
            pl.BlockSpec((None, d, de), lambda i, e: (e, 0, 0)),
            pl.BlockSpec((None, d, de), lambda i, e: (e, 0, 0)),
            pl.BlockSpec((None, de, d), lambda i, e: (e, 0, 0)),
            pl.BlockSpec((1, d), lambda i, e: (0, 0)),
        ],
        out_specs=pl.BlockSpec((tm, d), row),
        out_shape=jax.ShapeDtypeStruct((nt, d), F32),
        scratch_shapes=[pltpu.VMEM((tm, d), F32)],
        compiler_params=_params(("parallel", "arbitrary")),
        name="moe_dense",
    )(hn2, h1, gates, wg_b, wu_b, wd_b, norm_f)


def _rope_tables(pos):
    half = ROPE_DIM // 2
    inv = jnp.power(ROPE_THETA, -2.0 * jnp.arange(half, dtype=F32) / ROPE_DIM)
    ang = pos[:, None] * inv[None, :]
    cos, sin = jnp.cos(ang), jnp.sin(ang)
    n = pos.shape[0]
    pad = jnp.zeros((n, HEAD_QK - ROPE_DIM), F32)
    ct = jnp.concatenate([cos, cos, pad + 1.0], axis=1)
    s1 = jnp.concatenate([-sin, jnp.zeros_like(sin), pad], axis=1)
    s2 = jnp.concatenate([jnp.zeros_like(sin), sin, pad], axis=1)
    rep = LANES // HEAD_QK
    return jnp.stack([jnp.tile(t, (1, rep)) for t in (ct, s1, s2)])


def _block_diag_halves(w):
    nb, bs, _ = w.shape
    hb = nb // 2
    eye = jnp.eye(hb, dtype=w.dtype)
    halves = [jnp.einsum('nkj,nm->nkmj', w[i * hb:(i + 1) * hb], eye).reshape(hb * bs, hb * bs) for i in range(2)]
    return jnp.stack(halves).astype(BF16)


def kernel(x_prompt, x_sample, cache_k, cache_v, state_conv, state_h, page_table, meta_tokens, norm1, w_in,
           lambda_q1, lambda_k1, lambda_q2, lambda_k2, subln, conv_w, conv_b, rg_w_a, rg_b_a, rg_w_i, rg_b_i,
           lru_lambda, w_out, norm2, w_grp, b_grp, w_rt, b_rt, w_gate, w_up, w_down, norm_f):
    assert w_in.shape[0] == 1, "single-layer trunk only"
    n_batch, seq, d = x_prompt.shape
    n_seq, n_t, _ = x_sample.shape
    assert seq % ROW_TILE == 0 and n_t >= CONV_W - 1
    front = ROW_TILE - N_META
    lp = ROW_TILE + seq
    n_pages, page = page_table.shape[1], cache_k.shape[2]
    past_len = n_pages * page
    ns = n_seq * n_t
    ns_pad = -(-ns // ROW_TILE) * ROW_TILE
    np_rows = n_batch * lp
    rw = lru_lambda.shape[1]

    head = jnp.concatenate([jnp.zeros((front, d), F32), meta_tokens.astype(F32)], axis=0)
    xp = jnp.concatenate([jnp.broadcast_to(head[None], (n_batch, ROW_TILE, d)), x_prompt], axis=1)
    x_all = jnp.concatenate([xp.reshape(np_rows, d), x_sample.reshape(ns, d),
                             jnp.zeros((ns_pad - ns, d), F32)], axis=0)
    pos_p = jnp.maximum(jnp.arange(lp) - front, 0)
    pos_s = past_len + jnp.arange(n_t)
    pos = jnp.concatenate([jnp.tile(pos_p, n_batch), jnp.tile(pos_s, n_seq),
                           jnp.zeros((ns_pad - ns,), pos_p.dtype)]).astype(F32)
    cs = _rope_tables(pos)

    lamv = jnp.stack([lambda_q1[0], lambda_k1[0], lambda_q2[0], lambda_k2[0]]).astype(F32)
    gain = subln.astype(F32)

    q_all, k_all, kb_all, v_all, vb_all, xr_all, xg_all = _inproj(x_all, norm1, w_in[0].astype(BF16), cs)

    att_p = _prompt_attention(q_all, kb_all, vb_all, lamv, gain, n_batch, lp, front)
    wa_bd, wi_bd = _block_diag_halves(rg_w_a[0]), _block_diag_halves(rg_w_i[0])
    rnn_w = (conv_w[0], conv_b, wa_bd, rg_b_a, wi_bd, rg_b_i, lru_lambda)
    rnn_p, hl_p = _rglru_prompt(xr_all, xg_all, *rnn_w, n_batch, lp, front)

    sl = slice(np_rows, np_rows + ns)
    q_s = q_all[sl].reshape(n_seq, n_t, ATT_HEADS * 2, HEAD_QK)
    eye = jnp.eye(ATT_HEADS * 2, dtype=BF16)
    qbd = jnp.einsum('bthd,hg->bhtgd', q_s, eye).reshape(n_seq, ATT_HEADS * 2 * n_t, Q_WIDTH)
    att_s = _sample_attention(qbd, k_all[sl].reshape(n_seq, n_t, Q_WIDTH), v_all[sl].reshape(n_seq, n_t, ATT_WIDTH),
                              jnp.transpose(cache_k[0], (0, 2, 3, 4, 1)).reshape(-1, Q_WIDTH, page),
                              cache_v[0].reshape(-1, page * ATT_HEADS, HEAD_V),
                              page_table, lamv, gain)
    att_s = jnp.concatenate([att_s.reshape(ns, ATT_WIDTH).astype(BF16),
                             jnp.zeros((ns_pad - ns, ATT_WIDTH), BF16)], axis=0)
    to_tmajor = lambda a: jnp.swapaxes(a[sl].reshape(n_seq, n_t, rw), 0, 1)
    rnn_s, hl_s = _rglru_sample(to_tmajor(xr_all), to_tmajor(xg_all), jnp.swapaxes(state_conv[0], 0, 1),
                                state_h[0].astype(F32), *rnn_w)
    rnn_s = jnp.concatenate([jnp.swapaxes(rnn_s, 0, 1).reshape(ns, rw), jnp.zeros((ns_pad - ns, rw), BF16)], axis=0)

    w_router = jnp.concatenate([w_grp[0].T, w_rt[0].T,
                                jnp.zeros((32 - N_GROUPS - N_EXPERTS, d), F32)], axis=0).astype(F32)
    b_router = jnp.concatenate([b_grp[0], b_rt[0], jnp.zeros((32 - N_GROUPS - N_EXPERTS,), F32)])[:, None]
    h1, hn2, gates_t = _outproj(x_all, att_p, att_s, rnn_p, rnn_s, w_out[0].astype(BF16), norm2,
                                w_router, b_router.astype(F32))
    y_all = _moe(hn2, h1, gates_t.T, w_gate[0].astype(BF16), w_up[0].astype(BF16), w_down[0].astype(BF16),
                 norm_f[None, :])

    def prompt_rows(a):
        return a[:np_rows].reshape(n_batch, lp, -1)

    y_prompt = prompt_rows(y_all)[:, ROW_TILE:]
    y_sample = y_all[sl].reshape(n_seq, n_t, d)
    k_prompt = prompt_rows(k_all)[:, front:].reshape(1, n_batch, N_META + seq, ATT_HEADS, 2, HEAD_QK)
    v_prompt = prompt_rows(v_all)[:, front:].reshape(1, n_batch, N_META + seq, ATT_HEADS, HEAD_V)
    k_sample = k_all[sl].reshape(1, n_seq, n_t, ATT_HEADS, 2, HEAD_QK)
    v_sample = v_all[sl].reshape(1, n_seq, n_t, ATT_HEADS, HEAD_V)
    conv_prompt = prompt_rows(xr_all)[:, lp - (CONV_W - 1):][None]
    conv_sample = xr_all[sl].reshape(n_seq, n_t, rw)[:, n_t - (CONV_W - 1):][None]
    h_prompt = hl_p.reshape(n_batch, 8, rw)[:, 7][None]
    h_sample = hl_s[None]
    return (y_prompt, y_sample, k_prompt, v_prompt, k_sample, v_sample, conv_prompt, conv_sample,
            h_prompt, h_sample)
```

```python
import functools
import math

import jax
import jax.numpy as jnp
from jax import lax
from jax.experimental import pallas as pl
from jax.experimental.pallas import tpu as pltpu

F32 = jnp.float32
BF16 = jnp.bfloat16

N_META = 16
ATT_HEADS = 4
HEAD_QK = 64
HEAD_V = 2 * HEAD_QK
ATT_WIDTH = ATT_HEADS * HEAD_V
Q_WIDTH = ATT_HEADS * 2 * HEAD_QK
ROPE_DIM = HEAD_QK // 4
ROPE_THETA = 500000.0
RNN_BLOCKS = 8
CONV_W = 4
LRU_C = 8.0
N_GROUPS = 4
EXP_PER_GROUP = 4
N_EXPERTS = N_GROUPS * EXP_PER_GROUP
NORM_EPS = 1e-6
NEG_INF = -1e30
LAM_INIT = 0.8 - 0.6 * math.exp(-0.3 * 0)

LANES = 128
ROW_TILE = 256
VMEM_LIMIT = 56 * 1024 * 1024


def _params(sem, vmem=VMEM_LIMIT):
    return pltpu.CompilerParams(dimension_semantics=sem, vmem_limit_bytes=vmem)


def _sigmoid(x):
    return 1.0 / (1.0 + jnp.exp(-x))


def _softplus(x):
    return jnp.maximum(x, 0.0) + jnp.log1p(jnp.exp(-jnp.abs(x)))


def _gelu_tanh(x):
    return 0.5 * x * (1.0 + jnp.tanh(math.sqrt(2.0 / math.pi) * (x + 0.044715 * (x * x * x))))


def _lam_scalar(lamv):
    s1 = jnp.sum(lamv[0:1] * lamv[1:2], axis=-1, keepdims=True)
    s2 = jnp.sum(lamv[2:3] * lamv[3:4], axis=-1, keepdims=True)
    return jnp.exp(s1) - jnp.exp(s2) + LAM_INIT


class _TokenTiles:
    def __init__(self, n_batch, tiles_per_batch, n_sample_tiles):
        self.tpb = tiles_per_batch
        self.n_prompt = n_batch * tiles_per_batch
        self.n_sample = n_sample_tiles
        self.n_tiles = self.n_prompt + n_sample_tiles

    def prompt_block(self, i):
        blk = (i // self.tpb) * (self.tpb - 1) + jnp.maximum(i % self.tpb - 1, 0)
        return jnp.minimum(blk, self.n_prompt - self.n_prompt // self.tpb - 1)

    def sample_block(self, i):
        return jnp.clip(i - self.n_prompt, 0, self.n_sample - 1)

    def specs(self, d):
        tm = ROW_TILE
        return [pl.BlockSpec((tm, d), lambda i: (0, 0)),
                pl.BlockSpec((tm, d), lambda i: (self.prompt_block(i), 0)),
                pl.BlockSpec((tm, d), lambda i: (self.sample_block(i), 0))]

    def select(self, head_ref, xp_ref, xs_ref):
        i = pl.program_id(0)
        x = jnp.where(i % self.tpb == 0, head_ref[...], xp_ref[...])
        return jnp.where(i >= self.n_prompt, xs_ref[...], x)


def _inproj_body(head_ref, xp_ref, xs_ref, g_ref, w_ref, cs_ref, q_ref, k_ref, kb_ref, v_ref, vb_ref, xr_ref, xg_ref,
                 *, tiles):
    x = tiles.select(head_ref, xp_ref, xs_ref)
    ms = jnp.mean(x * x, axis=-1, keepdims=True)
    hn = (x * lax.rsqrt(ms + NORM_EPS) * g_ref[...]).astype(BF16)
    proj = jnp.dot(hn, w_ref[...], preferred_element_type=F32)
    ct, s1, s2 = cs_ref[0], cs_ref[1], cs_ref[2]

    def rope(xc):
        return xc * ct + pltpu.roll(xc, LANES - ROPE_DIM // 2, 1) * s1 + pltpu.roll(xc, ROPE_DIM // 2, 1) * s2

    for c in range(Q_WIDTH // LANES):
        sl = slice(c * LANES, (c + 1) * LANES)
        qc = rope(proj[:, c * LANES:(c + 1) * LANES])
        q_ref[:, sl] = (qc * (HEAD_QK ** -0.5)).astype(BF16)
        kc = rope(proj[:, Q_WIDTH + c * LANES:Q_WIDTH + (c + 1) * LANES])
        k_ref[:, sl] = kc
        kb_ref[:, sl] = kc.astype(BF16)
    v = proj[:, 2 * Q_WIDTH:2 * Q_WIDTH + ATT_WIDTH]
    v_ref[...] = v
    vb_ref[...] = v.astype(BF16)
    r0 = 2 * Q_WIDTH + ATT_WIDTH
    rw = xr_ref.shape[1]
    xr_ref[...] = proj[:, r0:r0 + rw]
    xg_ref[...] = proj[:, r0 + rw:r0 + 2 * rw]


def _inproj(tiles, head, xp, xs, norm1, w_in_b, cs):
    d = head.shape[1]
    tm = ROW_TILE
    nt = tiles.n_tiles * tm
    rw = (w_in_b.shape[1] - 2 * Q_WIDTH - ATT_WIDTH) // 2
    row = lambda i: (i, 0)
    const = lambda i: (0, 0)
    cs_tile = lambda i: (0, jnp.where(i < tiles.n_prompt, i % tiles.tpb, tiles.tpb), 0)
    outs = [
        jax.ShapeDtypeStruct((nt, Q_WIDTH), BF16),
        jax.ShapeDtypeStruct((nt, Q_WIDTH), F32),
        jax.ShapeDtypeStruct((nt, Q_WIDTH), BF16),
        jax.ShapeDtypeStruct((nt, ATT_WIDTH), F32),
        jax.ShapeDtypeStruct((nt, ATT_WIDTH), BF16),
        jax.ShapeDtypeStruct((nt, rw), F32),
        jax.ShapeDtypeStruct((nt, rw), F32),
    ]
    return pl.pallas_call(
        functools.partial(_inproj_body, tiles=tiles),
        grid=(tiles.n_tiles,),
        in_specs=tiles.specs(d) + [
            pl.BlockSpec((1, d), const),
            pl.BlockSpec(w_in_b.shape, const),
            pl.BlockSpec((3, tm, LANES), cs_tile),
        ],
        out_specs=[pl.BlockSpec((tm, o.shape[1]), row) for o in outs],
        out_shape=outs,
        compiler_params=_params(("parallel",)),
        name="inproj",
    )(head, xp, xs, norm1, w_in_b, cs)


def _flash_body(q_ref, k_ref, v_ref, lamv_ref, gain_ref, o_ref, q2_sc, m_sc, acc_sc, *, tq, front):
    qi = pl.program_id(1)
    lane = lax.broadcasted_iota(jnp.int32, (tq, LANES), 1)
    for h in range(ATT_HEADS):
        q = q_ref[:, h * LANES:(h + 1) * LANES]
        zero = jnp.zeros_like(q)
        q2_sc[h] = jnp.concatenate([jnp.where(lane < HEAD_QK, q, zero), jnp.where(lane >= HEAD_QK, q, zero)], axis=0)
    m_sc[...] = jnp.full(m_sc.shape, NEG_INF, F32)
    acc_sc[...] = jnp.zeros(acc_sc.shape, F32)
    ones = jnp.ones((tq, LANES), BF16)
    n_ch = tq // LANES

    def step(ki, masked):
        start = pl.multiple_of(ki * tq, tq)
        for h in range(ATT_HEADS):
            k = k_ref[pl.ds(start, tq), h * LANES:(h + 1) * LANES]
            v = v_ref[pl.ds(start, tq), h * LANES:(h + 1) * LANES]
            s = lax.dot_general(q2_sc[h], k, (((1,), (1,)), ((), ())), preferred_element_type=F32)
            if masked:
                row = lax.broadcasted_iota(jnp.int32, s.shape, 0)
                col = lax.broadcasted_iota(jnp.int32, s.shape, 1)
                qpos = qi * tq + jnp.where(row >= tq, row - tq, row)
                kpos = ki * tq + col
                s = jnp.where(kpos <= qpos, jnp.where(kpos >= front, s, NEG_INF), NEG_INF)
            chunks = [s[:, c * LANES:(c + 1) * LANES] for c in range(n_ch)]
            m_prev = m_sc[h]
            m_cur = jnp.max(functools.reduce(jnp.maximum, chunks), axis=1, keepdims=True)
            m_next = jnp.maximum(m_prev, m_cur)
            p = jnp.concatenate([jnp.exp(c - m_next) for c in chunks], axis=1).astype(BF16)
            alpha = jnp.exp(m_prev - m_next)
            pv = jnp.dot(p, jnp.concatenate([v, ones], axis=1), preferred_element_type=F32)
            acc_sc[h] = jnp.concatenate([alpha, alpha], axis=1) * acc_sc[h] + pv
            m_sc[h] = m_next

    step(0, True)

    def body(ki, carry):
        step(ki, False)
        return carry

    lax.fori_loop(1, qi, body, 0)

    @pl.when(qi > 0)
    def _():
        step(qi, True)

    lam = _lam_scalar(lamv_ref[...])
    for h in range(ATT_HEADS):
        acc = acc_sc[h]
        o = acc[:, :HEAD_V] / acc[:, HEAD_V:]
        d = o[:tq] - lam * o[tq:]
        ms = jnp.mean(d * d, axis=-1, keepdims=True)
        o_ref[:, h * HEAD_V:(h + 1) * HEAD_V] = (
            d * lax.rsqrt(ms + NORM_EPS) * gain_ref[...] * (1.0 - LAM_INIT)).astype(BF16)


def _prompt_attention(q_all, kb_all, vb_all, lamv, gain, n_batch, lp, front):
    tq = ROW_TILE
    nq = lp // tq
    kern = functools.partial(_flash_body, tq=tq, front=front)
    return pl.pallas_call(
        kern,
        grid=(n_batch, nq),
        in_specs=[
            pl.BlockSpec((tq, Q_WIDTH), lambda b, i: (b * nq + i, 0)),
            pl.BlockSpec((lp, Q_WIDTH), lambda b, i: (b, 0)),
            pl.BlockSpec((lp, ATT_WIDTH), lambda b, i: (b, 0)),
            pl.BlockSpec(lamv.shape, lambda b, i: (0, 0)),
            pl.BlockSpec(gain.shape, lambda b, i: (0, 0)),
        ],
        out_specs=pl.BlockSpec((tq, ATT_WIDTH), lambda b, i: (b * nq + i, 0)),
        out_shape=jax.ShapeDtypeStruct((n_batch * lp, ATT_WIDTH), BF16),
        scratch_shapes=[
            pltpu.VMEM((ATT_HEADS, 2 * tq, LANES), BF16),
            pltpu.VMEM((ATT_HEADS, 2 * tq, LANES), F32),
            pltpu.VMEM((ATT_HEADS, 2 * tq, 2 * HEAD_V), F32),
        ],
        compiler_params=_params(("parallel", "arbitrary")),
        name="prompt_attn",
    )(q_all, kb_all, vb_all, lamv, gain)


def _paged_body(pt_ref, qbd_ref, kn_ref, vn_ref, lamv_ref, gain_ref, *rest, n_pg, n_chunks, n_t):
    del pt_ref
    k_refs = rest[:n_pg]
    v_refs = rest[n_pg:2 * n_pg]
    o_ref = rest[2 * n_pg]
    m_sc, l_sc, acc_sc = rest[2 * n_pg + 1:]
    j = pl.program_id(1)
    n_rows = qbd_ref.shape[0]

    @pl.when(j == 0)
    def _():
        m_sc[...] = jnp.full(m_sc.shape, NEG_INF, F32)
        l_sc[...] = jnp.zeros(l_sc.shape, F32)
        acc_sc[...] = jnp.zeros(acc_sc.shape, F32)

    qbd = qbd_ref[...]
    chunks = [jnp.dot(qbd, k_refs[i][...].astype(BF16), preferred_element_type=F32) for i in range(n_pg)]
    m_prev = m_sc[...]
    m_cur = jnp.max(functools.reduce(jnp.maximum, chunks), axis=1, keepdims=True)
    m_next = jnp.maximum(m_prev, m_cur)
    p = [jnp.exp(c - m_next) for c in chunks]
    alpha = jnp.exp(m_prev - m_next)
    l_sc[...] = alpha * l_sc[...] + jnp.sum(functools.reduce(jnp.add, p), axis=1, keepdims=True)
    page = v_refs[0].shape[0] // ATT_HEADS
    hr = 2 * n_t
    pv = []
    for h in range(ATT_HEADS):
        d = None
        for i in range(n_pg):
            vh = v_refs[i][pl.ds(h, page, stride=ATT_HEADS), :].astype(BF16)
            t = jnp.dot(p[i][h * hr:(h + 1) * hr].astype(BF16), vh, preferred_element_type=F32)
            d = t if d is None else d + t
        pv.append(d)
    acc_sc[...] = acc_sc[...] * alpha + jnp.concatenate(pv, axis=0)
    m_sc[...] = m_next

    @pl.when(j == n_chunks - 1)
    def _():
        qf = qbd.astype(F32)
        kn = kn_ref[...].astype(BF16).astype(F32)
        vn = vn_ref[...].astype(BF16).astype(F32)
        row_t = lax.broadcasted_iota(jnp.int32, (n_rows, 1), 0) % n_t
        s_new = []
        for t2 in range(n_t):
            st = jnp.sum(qf * kn[t2:t2 + 1, :], axis=1, keepdims=True)
            s_new.append(jnp.where(row_t >= t2, st, NEG_INF))
        m_old = m_sc[...]
        m_fin = m_old
        for st in s_new:
            m_fin = jnp.maximum(m_fin, st)
        a_fin = jnp.exp(m_old - m_fin)
        l_fin = a_fin * l_sc[...]
        acc = acc_sc[...] * a_fin
        for t2 in range(n_t):
            pt = jnp.exp(s_new[t2] - m_fin)
            l_fin = l_fin + pt
            v_rows = jnp.concatenate(
                [jnp.broadcast_to(vn[t2:t2 + 1, h * HEAD_V:(h + 1) * HEAD_V], (2 * n_t, HEAD_V))
                 for h in range(ATT_HEADS)], axis=0)
            acc = acc + pt * v_rows
        o = acc / l_fin
        lam = _lam_scalar(lamv_ref[...])
        for h in range(ATT_HEADS):
            r0 = h * 2 * n_t
            d = o[r0:r0 + n_t] - lam * o[r0 + n_t:r0 + 2 * n_t]
            ms = jnp.mean(d * d, axis=-1, keepdims=True)
            o_ref[:, h * HEAD_V:(h + 1) * HEAD_V] = d * lax.rsqrt(ms + NORM_EPS) * gain_ref[...] * (1.0 - LAM_INIT)


def _sample_attention(qbd, k_new, v_new, cache_kt, cache_vm, page_table, lamv, gain):
    n_seq, n_rows, kw = qbd.shape
    n_t = k_new.shape[1]
    n_pages = page_table.shape[1]
    n_pg = math.gcd(n_pages, 16)
    n_chunks = n_pages // n_pg
    pt_flat = page_table.reshape(-1)

    def page_map(i):
        return lambda b, j, pt: (pt[b * n_pages + j * n_pg + i], 0, 0)

    seq3 = lambda b, j, pt: (b, 0, 0)
    const2 = lambda b, j, pt: (0, 0)
    kern = functools.partial(_paged_body, n_pg=n_pg, n_chunks=n_chunks, n_t=n_t)
    grid_spec = pltpu.PrefetchScalarGridSpec(
        num_scalar_prefetch=1,
        grid=(n_seq, n_chunks),
        in_specs=[
            pl.BlockSpec((None, n_rows, kw), seq3),
            pl.BlockSpec((None, n_t, kw), seq3),
            pl.BlockSpec((None, n_t, kw), seq3),
            pl.BlockSpec(lamv.shape, const2),
            pl.BlockSpec(gain.shape, const2),
        ] + [pl.BlockSpec((None,) + cache_kt.shape[1:], page_map(i)) for i in range(n_pg)]
          + [pl.BlockSpec((None,) + cache_vm.shape[1:], page_map(i)) for i in range(n_pg)],
        out_specs=pl.BlockSpec((None, n_t, ATT_WIDTH), seq3),
        scratch_shapes=[
            pltpu.VMEM((n_rows, LANES), F32),
            pltpu.VMEM((n_rows, LANES), F32),
            pltpu.VMEM((n_rows, HEAD_V), F32),
        ],
    )
    return pl.pallas_call(
        kern,
        grid_spec=grid_spec,
        out_shape=jax.ShapeDtypeStruct((n_seq, n_t, ATT_WIDTH), F32),
        compiler_params=_params(("parallel", "arbitrary")),
        name="sample_attn",
    )(pt_flat, qbd, k_new, v_new, lamv, gain, *([cache_kt] * n_pg), *([cache_vm] * n_pg))


def _lru_gates(xc, wa_ref, ba, wi_ref, bi, lam_row):
    half = xc.shape[1] // 2
    xb = xc.astype(BF16)

    def blockdiag(w_ref):
        return jnp.concatenate([
            jnp.dot(xb[:, :half], w_ref[0], preferred_element_type=F32),
            jnp.dot(xb[:, half:], w_ref[1], preferred_element_type=F32)], axis=1)

    gate_r = _sigmoid(blockdiag(wa_ref) + ba)
    gate_i = _sigmoid(blockdiag(wi_ref) + bi)
    log_a = -LRU_C * gate_r * _softplus(-lam_row)
    a = jnp.exp(log_a)
    th = jnp.tanh(log_a)
    u = jnp.sqrt(-2.0 * th / (1.0 - th)) * (gate_i * xc)
    return a, u


def _rglru_prompt_body(xr_ref, xg_ref, cw_ref, cb_ref, wa_ref, ba_ref, wi_ref, bi_ref, lam_ref,
                       rnn_ref, hl_ref, prev_sc, h_sc, *, tc, front):
    c = pl.program_id(1)

    @pl.when(c == 0)
    def _():
        prev_sc[...] = jnp.zeros(prev_sc.shape, F32)
        h_sc[...] = jnp.zeros(h_sc.shape, F32)

    x = xr_ref[...]
    xfull = jnp.concatenate([prev_sc[...], x], axis=0)
    cw = cw_ref[...]
    y = cb_ref[...] + xfull[5:5 + tc] * cw[0:1]
    y = y + xfull[6:6 + tc] * cw[1:2]
    y = y + xfull[7:7 + tc] * cw[2:3]
    y = y + x * cw[3:4]
    prev_sc[...] = x[tc - 8:tc]
    a, u = _lru_gates(y, wa_ref, ba_ref[...], wi_ref, bi_ref[...], lam_ref[...])
    row = lax.broadcasted_iota(jnp.int32, (tc, 1), 0)
    u = jnp.where(c * tc + row >= front, u, 0.0)
    d = 1
    while d < tc:
        keep = row >= d
        u = jnp.where(keep, a * pltpu.roll(u, d, 0) + u, u)
        a = jnp.where(keep, a * pltpu.roll(a, d, 0), a)
        d *= 2
    h = a * h_sc[0:1, :] + u
    h_sc[...] = h[tc - 8:tc]
    h_sc[0:1, :] = h[tc - 1:tc]
    hl_ref[...] = h[tc - 8:tc]
    rnn_ref[...] = (_gelu_tanh(xg_ref[...]) * h).astype(BF16)


def _rglru_prompt(xr_all, xg_all, cw, cb, wa_bd, ba, wi_bd, bi, lam, n_batch, lp, front):
    tc = ROW_TILE
    nc = lp // tc
    rw = xr_all.shape[1]
    kern = functools.partial(_rglru_prompt_body, tc=tc, front=front)
    row = lambda b, c: (b * nc + c, 0)
    const2 = lambda b, c: (0, 0)
    const3 = lambda b, c: (0, 0, 0)
    return pl.pallas_call(
        kern,
        grid=(n_batch, nc),
        in_specs=[
            pl.BlockSpec((tc, rw), row),
            pl.BlockSpec((tc, rw), row),
            pl.BlockSpec(cw.shape, const2),
            pl.BlockSpec(cb.shape, const2),
            pl.BlockSpec(wa_bd.shape, const3),
            pl.BlockSpec(ba.shape, const2),
            pl.BlockSpec(wi_bd.shape, const3),
            pl.BlockSpec(bi.shape, const2),
            pl.BlockSpec(lam.shape, const2),
        ],
        out_specs=[
            pl.BlockSpec((tc, rw), row),
            pl.BlockSpec((8, rw), lambda b, c: (b, 0)),
        ],
        out_shape=[
            jax.ShapeDtypeStruct((n_batch * lp, rw), BF16),
            jax.ShapeDtypeStruct((n_batch * 8, rw), F32),
        ],
        scratch_shapes=[pltpu.VMEM((8, rw), F32), pltpu.VMEM((8, rw), F32)],
        compiler_params=_params(("parallel", "arbitrary")),
        name="rglru_prompt",
    )(xr_all, xg_all, cw, cb, wa_bd, ba, wi_bd, bi, lam)


def _rglru_sample_body(xr_ref, xg_ref, cs_ref, h0_ref, cw_ref, cb_ref, wa_ref, ba_ref, wi_ref, bi_ref, lam_ref,
                       rnn_ref, hl_ref):
    n_t = xr_ref.shape[0]
    n_c = cs_ref.shape[0]
    rows = [cs_ref[i] for i in range(n_c)] + [xr_ref[t] for t in range(n_t)]
    cw = cw_ref[...]
    h = h0_ref[...]
    for t in range(n_t):
        y = cb_ref[...] + rows[t] * cw[0:1]
        for j in range(1, CONV_W):
            y = y + rows[t + j] * cw[j:j + 1]
        a, u = _lru_gates(y, wa_ref, ba_ref[...], wi_ref, bi_ref[...], lam_ref[...])
        h = a * h + u
        rnn_ref[t] = (_gelu_tanh(xg_ref[t]) * h).astype(BF16)
    hl_ref[...] = h


def _rglru_sample(xr_t, xg_t, conv_t, h0, cw, cb, wa_bd, ba, wi_bd, bi, lam):
    n_t, n_seq, rw = xr_t.shape
    return pl.pallas_call(
        _rglru_sample_body,
        out_shape=[
            jax.ShapeDtypeStruct((n_t, n_seq, rw), BF16),
            jax.ShapeDtypeStruct((n_seq, rw), F32),
        ],
        compiler_params=pltpu.CompilerParams(vmem_limit_bytes=VMEM_LIMIT),
        name="rglru_sample",
    )(xr_t, xg_t, conv_t, h0, cw, cb, wa_bd, ba, wi_bd, bi, lam)


def _route(logits):
    g = [logits[i:i + 1] for i in range(N_GROUPS)]
    gmax = functools.reduce(jnp.maximum, g)
    g_idx = jnp.full(gmax.shape, N_GROUPS - 1, jnp.int32)
    for i in range(N_GROUPS - 2, -1, -1):
        g_idx = jnp.where(g[i] == gmax, i, g_idx)
    g_sel = 1.0 / functools.reduce(jnp.add, [jnp.exp(gi - gmax) for gi in g])
    e = []
    for j in range(EXP_PER_GROUP):
        ej = logits[N_GROUPS + j:N_GROUPS + j + 1]
        for grp in range(1, N_GROUPS):
            r = N_GROUPS + grp * EXP_PER_GROUP + j
            ej = jnp.where(g_idx == grp, logits[r:r + 1], ej)
        e.append(ej)
    v1 = functools.reduce(jnp.maximum, e)
    i1 = jnp.full(v1.shape, EXP_PER_GROUP - 1, jnp.int32)
    for j in range(EXP_PER_GROUP - 2, -1, -1):
        i1 = jnp.where(e[j] == v1, j, i1)
    rest = [jnp.where(i1 == j, -jnp.inf, e[j]) for j in range(EXP_PER_GROUP)]
    v2 = functools.reduce(jnp.maximum, rest)
    i2 = jnp.full(v2.shape, EXP_PER_GROUP - 1, jnp.int32)
    for j in range(EXP_PER_GROUP - 2, -1, -1):
        i2 = jnp.where(rest[j] == v2, j, i2)
    z = jnp.exp(v2 - v1)
    w1 = g_sel / (1.0 + z)
    w2 = g_sel * z / (1.0 + z)
    e1 = g_idx * EXP_PER_GROUP + i1
    e2 = g_idx * EXP_PER_GROUP + i2
    rows = [jnp.where(e1 == x, w1, 0.0) + jnp.where(e2 == x, w2, 0.0) for x in range(N_EXPERTS)]
    return jnp.concatenate(rows, axis=0)


def _outproj_body(head_ref, xp_ref, xs_ref, attp_ref, atts_ref, rnnp_ref, rnns_ref, wo_ref, g2_ref, wr_ref, br_ref,
                  h1_ref, hn_ref, gates_ref, *, tiles):
    is_prompt = pl.program_id(0) < tiles.n_prompt
    aw = attp_ref.shape[1]
    att = jnp.where(is_prompt, attp_ref[...], atts_ref[...])
    rnn = jnp.where(is_prompt, rnnp_ref[...], rnns_ref[...])
    mix = jnp.dot(att, wo_ref[0:aw, :], preferred_element_type=F32)
    mix = mix + jnp.dot(rnn, wo_ref[aw:, :], preferred_element_type=F32)
    h1 = tiles.select(head_ref, xp_ref, xs_ref) + mix
    h1_ref[...] = h1
    ms = jnp.mean(h1 * h1, axis=-1, keepdims=True)
    hn = h1 * lax.rsqrt(ms + NORM_EPS) * g2_ref[...]
    hn_ref[...] = hn.astype(BF16)
    logits = lax.dot_general(wr_ref[...], hn, (((1,), (1,)), ((), ())),
                             precision=lax.Precision.HIGHEST, preferred_element_type=F32) + br_ref[...]
    gates_ref[...] = _route(logits)


def _outproj(tiles, head, xp, xs, att_p, att_s, rnn_p, rnn_s, w_out_b, norm2, w_router, b_router):
    d = head.shape[1]
    tm = ROW_TILE
    nt = tiles.n_tiles * tm
    npt = tiles.n_prompt
    row = lambda i: (i, 0)
    const = lambda i: (0, 0)
    prow = lambda i: (jnp.minimum(i, npt - 1), 0)
    srow = lambda i: (tiles.sample_block(i), 0)
    kern = functools.partial(_outproj_body, tiles=tiles)
    return pl.pallas_call(
        kern,
        grid=(tiles.n_tiles,),
        in_specs=tiles.specs(d) + [
            pl.BlockSpec((tm, att_p.shape[1]), prow),
            pl.BlockSpec((tm, att_s.shape[1]), srow),
            pl.BlockSpec((tm, rnn_p.shape[1]), prow),
            pl.BlockSpec((tm, rnn_s.shape[1]), srow),
            pl.BlockSpec(w_out_b.shape, const),
            pl.BlockSpec(norm2.shape, const),
            pl.BlockSpec(w_router.shape, const),
            pl.BlockSpec(b_router.shape, const),
        ],
        out_specs=[
            pl.BlockSpec((tm, d), row),
            pl.BlockSpec((tm, d), row),
            pl.BlockSpec((N_EXPERTS, tm), lambda i: (0, i)),
        ],
        out_shape=[
            jax.ShapeDtypeStruct((nt, d), F32),
            jax.ShapeDtypeStruct((nt, d), BF16),
            jax.ShapeDtypeStruct((N_EXPERTS, nt), F32),
        ],
        compiler_params=_params(("parallel",)),
        name="outproj_router",
    )(head, xp, xs, att_p, att_s, rnn_p, rnn_s, w_out_b, norm2, w_router, b_router)


def _moe_body(hn_ref, h1_ref, gates_ref, wg_ref, wu_ref, wd_ref, nf_ref, y_ref, acc_sc):
    e = pl.program_id(1)

    @pl.when(e == 0)
    def _():
        acc_sc[...] = jnp.zeros(acc_sc.shape, F32)

    x = hn_ref[...]
    hg = jnp.dot(x, wg_ref[...], preferred_element_type=F32)
    hu = jnp.dot(x, wu_ref[...], preferred_element_type=F32)
    he = (hg * _sigmoid(hg) * hu).astype(BF16)
    out = jnp.dot(he, wd_ref[...], preferred_element_type=F32)
    gates = gates_ref[...]
    lane = lax.broadcasted_iota(jnp.int32, gates.shape, 1)
    gcol = jnp.sum(jnp.where(lane == e, gates, 0.0), axis=1, keepdims=True)
    acc_sc[...] = acc_sc[...] + gcol * out

    @pl.when(e == pl.num_programs(1) - 1)
    def _():
        h2 = h1_ref[...] + acc_sc[...]
        ms = jnp.mean(h2 * h2, axis=-1, keepdims=True)
        y_ref[...] = h2 * lax.rsqrt(ms + NORM_EPS) * nf_ref[...]


def _moe(hn2, h1, gates, wg_b, wu_b, wd_b, norm_f):
    nt, d = h1.shape
    de = wg_b.shape[2]
    tm = 512 if nt % 512 == 0 else ROW_TILE
    row = lambda i, e: (i, 0)
    return pl.pallas_call(
        _moe_body,
        grid=(nt // tm, N_EXPERTS),
        in_specs=[
            pl.BlockSpec((tm, d), row),
            pl.BlockSpec((tm, d), row),
            pl.BlockSpec((tm, N_EXPERTS), row),
            pl.BlockSpec((None, d, de), lambda i, e: (e, 0, 0)),
            pl.BlockSpec((None, d, de), lambda i, e: (e, 0, 0)),
            pl.BlockSpec((None, de, d), lambda i, e: (e, 0, 0)),
            pl.BlockSpec((1, d), lambda i, e: (0, 0)),
        ],
        out_specs=pl.BlockSpec((tm, d), row),
        out_shape=jax.ShapeDtypeStruct((nt, d), F32),
        scratch_shapes=[pltpu.VMEM((tm, d), F32)],
        compiler_params=_params(("parallel", "arbitrary")),
        name="moe_dense",
    )(hn2, h1, gates, wg_b, wu_b, wd_b, norm_f)


def _rope_tables(pos):
    half = ROPE_DIM // 2
    inv = jnp.power(ROPE_THETA, -2.0 * jnp.arange(half, dtype=F32) / ROPE_DIM)
    ang = pos[:, None] * inv[None, :]
    cos, sin = jnp.cos(ang), jnp.sin(ang)
    n = pos.shape[0]
    pad = jnp.zeros((n, HEAD_QK - ROPE_DIM), F32)
    ct = jnp.concatenate([cos, cos, pad + 1.0], axis=1)
    s1 = jnp.concatenate([-sin, jnp.zeros_like(sin), pad], axis=1)
    s2 = jnp.concatenate([jnp.zeros_like(sin), sin, pad], axis=1)
    rep = LANES // HEAD_QK
    return jnp.stack([jnp.tile(t, (1, rep)) for t in (ct, s1, s2)])


def _block_diag_halves(w):
    nb, bs, _ = w.shape
    hb = nb // 2
    eye = jnp.eye(hb, dtype=w.dtype)
    halves = [jnp.einsum('nkj,nm->nkmj', w[i * hb:(i + 1) * hb], eye).reshape(hb * bs, hb * bs) for i in range(2)]
    return jnp.stack(halves).astype(BF16)


def kernel(x_prompt, x_sample, cache_k, cache_v, state_conv, state_h, page_table, meta_tokens, norm1, w_in,
           lambda_q1, lambda_k1, lambda_q2, lambda_k2, subln, conv_w, conv_b, rg_w_a, rg_b_a, rg_w_i, rg_b_i,
           lru_lambda, w_out, norm2, w_grp, b_grp, w_rt, b_rt, w_gate, w_up, w_down, norm_f):
    assert w_in.shape[0] == 1, "single-layer trunk only"
    n_batch, seq, d = x_prompt.shape
    n_seq, n_t, _ = x_sample.shape
    assert seq % ROW_TILE == 0 and n_t >= CONV_W - 1
    front = ROW_TILE - N_META
    lp = ROW_TILE + seq
    n_pages, page = page_table.shape[1], cache_k.shape[2]
    past_len = n_pages * page
    ns = n_seq * n_t
    ns_pad = -(-ns // ROW_TILE) * ROW_TILE
    np_rows = n_batch * lp
    rw = lru_lambda.shape[1]

    assert ROW_TILE % n_t == 0
    tiles = _TokenTiles(n_batch, lp // ROW_TILE, ns_pad // ROW_TILE)
    head = jnp.concatenate([jnp.zeros((front, d), F32), meta_tokens.astype(F32)], axis=0)
    xp = x_prompt.reshape(n_batch * seq, d)
    xs = jnp.pad(x_sample.reshape(ns, d), ((0, ns_pad - ns), (0, 0)))
    pos_p = jnp.maximum(jnp.arange(lp) - front, 0)
    pos_s = past_len + jnp.arange(ROW_TILE) % n_t
    cs = _rope_tables(jnp.concatenate([pos_p, pos_s]).astype(F32))

    lamv = jnp.stack([lambda_q1[0], lambda_k1[0], lambda_q2[0], lambda_k2[0]]).astype(F32)
    gain = subln.astype(F32)

    q_all, k_all, kb_all, v_all, vb_all, xr_all, xg_all = _inproj(tiles, head, xp, xs, norm1,
                                                                  w_in[0].astype(BF16), cs)

    att_p = _prompt_attention(q_all, kb_all, vb_all, lamv, gain, n_batch, lp, front)
    wa_bd, wi_bd = _block_diag_halves(rg_w_a[0]), _block_diag_halves(rg_w_i[0])
    rnn_w = (conv_w[0], conv_b, wa_bd, rg_b_a, wi_bd, rg_b_i, lru_lambda)
    rnn_p, hl_p = _rglru_prompt(xr_all, xg_all, *rnn_w, n_batch, lp, front)

    sl = slice(np_rows, np_rows + ns)
    q_s = q_all[sl].reshape(n_seq, n_t, ATT_HEADS * 2, HEAD_QK)
    eye = jnp.eye(ATT_HEADS * 2, dtype=BF16)
    qbd = jnp.einsum('bthd,hg->bhtgd', q_s, eye).reshape(n_seq, ATT_HEADS * 2 * n_t, Q_WIDTH)
    att_s = _sample_attention(qbd, k_all[sl].reshape(n_seq, n_t, Q_WIDTH), v_all[sl].reshape(n_seq, n_t, ATT_WIDTH),
                              jnp.transpose(cache_k[0], (0, 2, 3, 4, 1)).reshape(-1, Q_WIDTH, page),
                              cache_v[0].reshape(-1, page * ATT_HEADS, HEAD_V),
                              page_table, lamv, gain)
    att_s = jnp.concatenate([att_s.reshape(ns, ATT_WIDTH).astype(BF16),
                             jnp.zeros((ns_pad - ns, ATT_WIDTH), BF16)], axis=0)
    to_tmajor = lambda a: jnp.swapaxes(a[sl].reshape(n_seq, n_t, rw), 0, 1)
    rnn_s, hl_s = _rglru_sample(to_tmajor(xr_all), to_tmajor(xg_all), jnp.swapaxes(state_conv[0], 0, 1),
                                state_h[0].astype(F32), *rnn_w)
    rnn_s = jnp.concatenate([jnp.swapaxes(rnn_s, 0, 1).reshape(ns, rw), jnp.zeros((ns_pad - ns, rw), BF16)], axis=0)

    w_router = jnp.concatenate([w_grp[0].T, w_rt[0].T,
                                jnp.zeros((32 - N_GROUPS - N_EXPERTS, d), F32)], axis=0).astype(F32)
    b_router = jnp.concatenate([b_grp[0], b_rt[0], jnp.zeros((32 - N_GROUPS - N_EXPERTS,), F32)])[:, None]
    h1, hn2, gates_t = _outproj(tiles, head, xp, xs, att_p, att_s, rnn_p, rnn_s, w_out[0].astype(BF16), norm2,
                                w_router, b_router.astype(F32))
    y_all = _moe(hn2, h1, gates_t.T, w_gate[0].astype(BF16), w_up[0].astype(BF16), w_down[0].astype(BF16),
                 norm_f[None, :])

    def prompt_rows(a):
        return a[:np_rows].reshape(n_batch, lp, -1)

    y_prompt = prompt_rows(y_all)[:, ROW_TILE:]
    y_sample = y_all[sl].reshape(n_seq, n_t, d)
    k_prompt = prompt_rows(k_all)[:, front:].reshape(1, n_batch, N_META + seq, ATT_HEADS, 2, HEAD_QK)
    v_prompt = prompt_rows(v_all)[:, front:].reshape(1, n_batch, N_META + seq, ATT_HEADS, HEAD_V)
    k_sample = k_all[sl].reshape(1, n_seq, n_t, ATT_HEADS, 2, HEAD_QK)
    v_sample = v_all[sl].reshape(1, n_seq, n_t, ATT_HEADS, HEAD_V)
    conv_prompt = prompt_rows(xr_all)[:, lp - (CONV_W - 1):][None]
    conv_sample = xr_all[sl].reshape(n_seq, n_t, rw)[:, n_t - (CONV_W - 1):][None]
    h_prompt = hl_p.reshape(n_batch, 8, rw)[:, 7][None]
    h_sample = hl_s[None]
    return (y_prompt, y_sample, k_prompt, v_prompt, k_sample, v_sample, conv_prompt, conv_sample,
            h_prompt, h_sample)
```

```python
import functools
import math

import jax
import jax.numpy as jnp
from jax import lax
from jax.experimental import pallas as pl
from jax.experimental.pallas import tpu as pltpu

F32 = jnp.float32
BF16 = jnp.bfloat16

N_META = 16
ATT_HEADS = 4
HEAD_QK = 64
HEAD_V = 2 * HEAD_QK
ATT_WIDTH = ATT_HEADS * HEAD_V
Q_WIDTH = ATT_HEADS * 2 * HEAD_QK
ROPE_DIM = HEAD_QK // 4
ROPE_THETA = 500000.0
RNN_BLOCKS = 8
CONV_W = 4
LRU_C = 8.0
N_GROUPS = 4
EXP_PER_GROUP = 4
N_EXPERTS = N_GROUPS * EXP_PER_GROUP
NORM_EPS = 1e-6
NEG_INF = -1e30
LAM_INIT = 0.8 - 0.6 * math.exp(-0.3 * 0)

LANES = 128
ROW_TILE = 256
VMEM_LIMIT = 56 * 1024 * 1024


def _params(sem, vmem=VMEM_LIMIT):
    return pltpu.CompilerParams(dimension_semantics=sem, vmem_limit_bytes=vmem)


def _sigmoid(x):
    return 1.0 / (1.0 + jnp.exp(-x))


def _softplus(x):
    return jnp.maximum(x, 0.0) + jnp.log1p(jnp.exp(-jnp.abs(x)))


def _gelu_tanh(x):
    return 0.5 * x * (1.0 + jnp.tanh(math.sqrt(2.0 / math.pi) * (x + 0.044715 * (x * x * x))))


def _lam_scalar(lamv):
    s1 = jnp.sum(lamv[0:1] * lamv[1:2], axis=-1, keepdims=True)
    s2 = jnp.sum(lamv[2:3] * lamv[3:4], axis=-1, keepdims=True)
    return jnp.exp(s1) - jnp.exp(s2) + LAM_INIT


class _TokenTiles:
    def __init__(self, n_batch, tiles_per_batch, n_sample_tiles):
        self.tpb = tiles_per_batch
        self.n_prompt = n_batch * tiles_per_batch
        self.n_sample = n_sample_tiles
        self.n_tiles = self.n_prompt + n_sample_tiles

    def prompt_block(self, i):
        blk = (i // self.tpb) * (self.tpb - 1) + jnp.maximum(i % self.tpb - 1, 0)
        return jnp.minimum(blk, self.n_prompt - self.n_prompt // self.tpb - 1)

    def sample_block(self, i):
        return jnp.clip(i - self.n_prompt, 0, self.n_sample - 1)

    def specs(self, d):
        tm = ROW_TILE
        return [pl.BlockSpec((tm, d), lambda i: (0, 0)),
                pl.BlockSpec((tm, d), lambda i: (self.prompt_block(i), 0)),
                pl.BlockSpec((tm, d), lambda i: (self.sample_block(i), 0))]

    def select(self, head_ref, xp_ref, xs_ref):
        i = pl.program_id(0)
        x = jnp.where(i % self.tpb == 0, head_ref[...], xp_ref[...])
        return jnp.where(i >= self.n_prompt, xs_ref[...], x)


def _inproj_body(head_ref, xp_ref, xs_ref, g_ref, w_ref, cs_ref, q_ref, k_ref, kb_ref, v_ref, vb_ref, xr_ref, xg_ref,
                 *, tiles):
    x = tiles.select(head_ref, xp_ref, xs_ref)
    ms = jnp.mean(x * x, axis=-1, keepdims=True)
    hn = (x * lax.rsqrt(ms + NORM_EPS) * g_ref[...]).astype(BF16)
    proj = jnp.dot(hn, w_ref[...], preferred_element_type=F32)
    ct, s1, s2 = cs_ref[0], cs_ref[1], cs_ref[2]

    def rope(xc):
        return xc * ct + pltpu.roll(xc, LANES - ROPE_DIM // 2, 1) * s1 + pltpu.roll(xc, ROPE_DIM // 2, 1) * s2

    for c in range(Q_WIDTH // LANES):
        sl = slice(c * LANES, (c + 1) * LANES)
        qc = rope(proj[:, c * LANES:(c + 1) * LANES])
        q_ref[:, sl] = (qc * (HEAD_QK ** -0.5)).astype(BF16)
        kc = rope(proj[:, Q_WIDTH + c * LANES:Q_WIDTH + (c + 1) * LANES])
        k_ref[:, sl] = kc
        kb_ref[:, sl] = kc.astype(BF16)
    v = proj[:, 2 * Q_WIDTH:2 * Q_WIDTH + ATT_WIDTH]
    v_ref[...] = v
    vb_ref[...] = v.astype(BF16)
    r0 = 2 * Q_WIDTH + ATT_WIDTH
    rw = xr_ref.shape[1]
    xr_ref[...] = proj[:, r0:r0 + rw]
    xg_ref[...] = proj[:, r0 + rw:r0 + 2 * rw]


def _inproj(tiles, head, xp, xs, norm1, w_in_b, cs):
    d = head.shape[1]
    tm = ROW_TILE
    nt = tiles.n_tiles * tm
    rw = (w_in_b.shape[1] - 2 * Q_WIDTH - ATT_WIDTH) // 2
    row = lambda i: (i, 0)
    const = lambda i: (0, 0)
    cs_tile = lambda i: (0, jnp.where(i < tiles.n_prompt, i % tiles.tpb, tiles.tpb), 0)
    outs = [
        jax.ShapeDtypeStruct((nt, Q_WIDTH), BF16),
        jax.ShapeDtypeStruct((nt, Q_WIDTH), F32),
        jax.ShapeDtypeStruct((nt, Q_WIDTH), BF16),
        jax.ShapeDtypeStruct((nt, ATT_WIDTH), F32),
        jax.ShapeDtypeStruct((nt, ATT_WIDTH), BF16),
        jax.ShapeDtypeStruct((nt, rw), F32),
        jax.ShapeDtypeStruct((nt, rw), F32),
    ]
    return pl.pallas_call(
        functools.partial(_inproj_body, tiles=tiles),
        grid=(tiles.n_tiles,),
        in_specs=tiles.specs(d) + [
            pl.BlockSpec((1, d), const),
            pl.BlockSpec(w_in_b.shape, const),
            pl.BlockSpec((3, tm, LANES), cs_tile),
        ],
        out_specs=[pl.BlockSpec((tm, o.shape[1]), row) for o in outs],
        out_shape=outs,
        compiler_params=_params(("parallel",)),
        name="inproj",
    )(head, xp, xs, norm1, w_in_b, cs)


def _flash_body(q_ref, k_ref, v_ref, lamv_ref, gain_ref, o_ref, q2_sc, m_sc, acc_sc, *, tq, front):
    qi = pl.program_id(1)
    lane = lax.broadcasted_iota(jnp.int32, (tq, LANES), 1)
    for h in range(ATT_HEADS):
        q = q_ref[:, h * LANES:(h + 1) * LANES]
        zero = jnp.zeros_like(q)
        q2_sc[h] = jnp.concatenate([jnp.where(lane < HEAD_QK, q, zero), jnp.where(lane >= HEAD_QK, q, zero)], axis=0)
    m_sc[...] = jnp.full(m_sc.shape, NEG_INF, F32)
    acc_sc[...] = jnp.zeros(acc_sc.shape, F32)

    def step(ki, width, masked):
        start = pl.multiple_of(ki * tq, tq)
        n_ch = width // LANES
        ones = jnp.ones((width, LANES), BF16)
        for h in range(ATT_HEADS):
            k = k_ref[pl.ds(start, width), h * LANES:(h + 1) * LANES]
            v = v_ref[pl.ds(start, width), h * LANES:(h + 1) * LANES]
            s = lax.dot_general(q2_sc[h], k, (((1,), (1,)), ((), ())), preferred_element_type=F32)
            if masked:
                row = lax.broadcasted_iota(jnp.int32, s.shape, 0)
                col = lax.broadcasted_iota(jnp.int32, s.shape, 1)
                qpos = qi * tq + jnp.where(row >= tq, row - tq, row)
                kpos = ki * tq + col
                s = jnp.where(kpos <= qpos, jnp.where(kpos >= front, s, NEG_INF), NEG_INF)
            chunks = [s[:, c * LANES:(c + 1) * LANES] for c in range(n_ch)]
            m_prev = m_sc[h]
            m_cur = jnp.max(functools.reduce(jnp.maximum, chunks), axis=1, keepdims=True)
            m_next = jnp.maximum(m_prev, m_cur)
            p = jnp.concatenate([jnp.exp(c - m_next) for c in chunks], axis=1).astype(BF16)
            alpha = jnp.exp(m_prev - m_next)
            pv = jnp.dot(p, jnp.concatenate([v, ones], axis=1), preferred_element_type=F32)
            acc_sc[h] = jnp.concatenate([alpha, alpha], axis=1) * acc_sc[h] + pv
            m_sc[h] = m_next

    step(0, tq, True)
    n_plain = jnp.maximum(qi - 1, 0)

    def body(j, carry):
        step(1 + 2 * j, 2 * tq, False)
        return carry

    lax.fori_loop(0, n_plain // 2, body, 0)

    @pl.when(n_plain % 2 == 1)
    def _():
        step(qi - 1, tq, False)

    @pl.when(qi > 0)
    def _():
        step(qi, tq, True)

    lam = _lam_scalar(lamv_ref[...])
    for h in range(ATT_HEADS):
        acc = acc_sc[h]
        o = acc[:, :HEAD_V] / acc[:, HEAD_V:]
        d = o[:tq] - lam * o[tq:]
        ms = jnp.mean(d * d, axis=-1, keepdims=True)
        o_ref[:, h * HEAD_V:(h + 1) * HEAD_V] = (
            d * lax.rsqrt(ms + NORM_EPS) * gain_ref[...] * (1.0 - LAM_INIT)).astype(BF16)


def _prompt_attention(q_all, kb_all, vb_all, lamv, gain, n_batch, lp, front):
    tq = ROW_TILE
    nq = lp // tq
    kern = functools.partial(_flash_body, tq=tq, front=front)
    return pl.pallas_call(
        kern,
        grid=(n_batch, nq),
        in_specs=[
            pl.BlockSpec((tq, Q_WIDTH), lambda b, i: (b * nq + i, 0)),
            pl.BlockSpec((lp, Q_WIDTH), lambda b, i: (b, 0)),
            pl.BlockSpec((lp, ATT_WIDTH), lambda b, i: (b, 0)),
            pl.BlockSpec(lamv.shape, lambda b, i: (0, 0)),
            pl.BlockSpec(gain.shape, lambda b, i: (0, 0)),
        ],
        out_specs=pl.BlockSpec((tq, ATT_WIDTH), lambda b, i: (b * nq + i, 0)),
        out_shape=jax.ShapeDtypeStruct((n_batch * lp, ATT_WIDTH), BF16),
        scratch_shapes=[
            pltpu.VMEM((ATT_HEADS, 2 * tq, LANES), BF16),
            pltpu.VMEM((ATT_HEADS, 2 * tq, LANES), F32),
            pltpu.VMEM((ATT_HEADS, 2 * tq, 2 * HEAD_V), F32),
        ],
        compiler_params=_params(("parallel", "arbitrary")),
        name="prompt_attn",
    )(q_all, kb_all, vb_all, lamv, gain)


def _paged_body(pt_ref, qbd_ref, kn_ref, vn_ref, lamv_ref, gain_ref, *rest, n_pg, n_chunks, n_t):
    del pt_ref
    k_refs = rest[:n_pg]
    v_refs = rest[n_pg:2 * n_pg]
    o_ref = rest[2 * n_pg]
    m_sc, l_sc, acc_sc = rest[2 * n_pg + 1:]
    j = pl.program_id(1)
    n_rows = qbd_ref.shape[0]

    @pl.when(j == 0)
    def _():
        m_sc[...] = jnp.full(m_sc.shape, NEG_INF, F32)
        l_sc[...] = jnp.zeros(l_sc.shape, F32)
        acc_sc[...] = jnp.zeros(acc_sc.shape, F32)

    qbd = qbd_ref[...]
    chunks = [jnp.dot(qbd, k_refs[i][...].astype(BF16), preferred_element_type=F32) for i in range(n_pg)]
    m_prev = m_sc[...]
    m_cur = jnp.max(functools.reduce(jnp.maximum, chunks), axis=1, keepdims=True)
    m_next = jnp.maximum(m_prev, m_cur)
    p = [jnp.exp(c - m_next) for c in chunks]
    alpha = jnp.exp(m_prev - m_next)
    l_sc[...] = alpha * l_sc[...] + jnp.sum(functools.reduce(jnp.add, p), axis=1, keepdims=True)
    page = v_refs[0].shape[0] // ATT_HEADS
    hr = 2 * n_t
    pv = []
    for h in range(ATT_HEADS):
        d = None
        for i in range(n_pg):
            vh = v_refs[i][pl.ds(h, page, stride=ATT_HEADS), :].astype(BF16)
            t = jnp.dot(p[i][h * hr:(h + 1) * hr].astype(BF16), vh, preferred_element_type=F32)
            d = t if d is None else d + t
        pv.append(d)
    acc_sc[...] = acc_sc[...] * alpha + jnp.concatenate(pv, axis=0)
    m_sc[...] = m_next

    @pl.when(j == n_chunks - 1)
    def _():
        qf = qbd.astype(F32)
        kn = kn_ref[...].astype(BF16).astype(F32)
        vn = vn_ref[...].astype(BF16).astype(F32)
        row_t = lax.broadcasted_iota(jnp.int32, (n_rows, 1), 0) % n_t
        s_new = []
        for t2 in range(n_t):
            st = jnp.sum(qf * kn[t2:t2 + 1, :], axis=1, keepdims=True)
            s_new.append(jnp.where(row_t >= t2, st, NEG_INF))
        m_old = m_sc[...]
        m_fin = m_old
        for st in s_new:
            m_fin = jnp.maximum(m_fin, st)
        a_fin = jnp.exp(m_old - m_fin)
        l_fin = a_fin * l_sc[...]
        acc = acc_sc[...] * a_fin
        for t2 in range(n_t):
            pt = jnp.exp(s_new[t2] - m_fin)
            l_fin = l_fin + pt
            v_rows = jnp.concatenate(
                [jnp.broadcast_to(vn[t2:t2 + 1, h * HEAD_V:(h + 1) * HEAD_V], (2 * n_t, HEAD_V))
                 for h in range(ATT_HEADS)], axis=0)
            acc = acc + pt * v_rows
        o = acc / l_fin
        lam = _lam_scalar(lamv_ref[...])
        for h in range(ATT_HEADS):
            r0 = h * 2 * n_t
            d = o[r0:r0 + n_t] - lam * o[r0 + n_t:r0 + 2 * n_t]
            ms = jnp.mean(d * d, axis=-1, keepdims=True)
            o_ref[:, h * HEAD_V:(h + 1) * HEAD_V] = d * lax.rsqrt(ms + NORM_EPS) * gain_ref[...] * (1.0 - LAM_INIT)


def _sample_attention(qbd, k_new, v_new, cache_kt, cache_vm, page_table, lamv, gain):
    n_seq, n_rows, kw = qbd.shape
    n_t = k_new.shape[1]
    n_pages = page_table.shape[1]
    n_pg = math.gcd(n_pages, 16)
    n_chunks = n_pages // n_pg
    pt_flat = page_table.reshape(-1)

    def page_map(i):
        return lambda b, j, pt: (pt[b * n_pages + j * n_pg + i], 0, 0)

    seq3 = lambda b, j, pt: (b, 0, 0)
    const2 = lambda b, j, pt: (0, 0)
    kern = functools.partial(_paged_body, n_pg=n_pg, n_chunks=n_chunks, n_t=n_t)
    grid_spec = pltpu.PrefetchScalarGridSpec(
        num_scalar_prefetch=1,
        grid=(n_seq, n_chunks),
        in_specs=[
            pl.BlockSpec((None, n_rows, kw), seq3),
            pl.BlockSpec((None, n_t, kw), seq3),
            pl.BlockSpec((None, n_t, kw), seq3),
            pl.BlockSpec(lamv.shape, const2),
            pl.BlockSpec(gain.shape, const2),
        ] + [pl.BlockSpec((None,) + cache_kt.shape[1:], page_map(i)) for i in range(n_pg)]
          + [pl.BlockSpec((None,) + cache_vm.shape[1:], page_map(i)) for i in range(n_pg)],
        out_specs=pl.BlockSpec((None, n_t, ATT_WIDTH), seq3),
        scratch_shapes=[
            pltpu.VMEM((n_rows, LANES), F32),
            pltpu.VMEM((n_rows, LANES), F32),
            pltpu.VMEM((n_rows, HEAD_V), F32),
        ],
    )
    return pl.pallas_call(
        kern,
        grid_spec=grid_spec,
        out_shape=jax.ShapeDtypeStruct((n_seq, n_t, ATT_WIDTH), F32),
        compiler_params=_params(("parallel", "arbitrary")),
        name="sample_attn",
    )(pt_flat, qbd, k_new, v_new, lamv, gain, *([cache_kt] * n_pg), *([cache_vm] * n_pg))


def _lru_gates(xc, wa_ref, ba, wi_ref, bi, lam_row):
    half = xc.shape[1] // 2
    xb = xc.astype(BF16)

    def blockdiag(w_ref):
        return jnp.concatenate([
            jnp.dot(xb[:, :half], w_ref[0], preferred_element_type=F32),
            jnp.dot(xb[:, half:], w_ref[1], preferred_element_type=F32)], axis=1)

    gate_r = _sigmoid(blockdiag(wa_ref) + ba)
    gate_i = _sigmoid(blockdiag(wi_ref) + bi)
    log_a = -LRU_C * gate_r * _softplus(-lam_row)
    a = jnp.exp(log_a)
    th = jnp.tanh(log_a)
    om = -2.0 * th / (1.0 - th)
    root = om * lax.rsqrt(jnp.maximum(om, jnp.finfo(jnp.float32).tiny))
    u = root * (gate_i * xc)
    return a, u


def _rglru_prompt_body(xr_ref, xg_ref, cw_ref, cb_ref, wa_ref, ba_ref, wi_ref, bi_ref, lam_ref,
                       rnn_ref, hl_ref, prev_sc, h_sc, *, tc, front):
    c = pl.program_id(1)

    @pl.when(c == 0)
    def _():
        prev_sc[...] = jnp.zeros(prev_sc.shape, F32)
        h_sc[...] = jnp.zeros(h_sc.shape, F32)

    x = xr_ref[...]
    prev = prev_sc[...]
    rw = x.shape[1]
    row8 = lax.broadcasted_iota(jnp.int32, (8, 1), 0)

    def delayed(d):
        r = pltpu.roll(x, d, 0)
        top = jnp.where(row8 < d, pltpu.roll(prev, d, 0), r[0:8])
        return jnp.concatenate([top, r[8:]], axis=0)

    cw = cw_ref[...]
    y = cb_ref[...] + delayed(3) * cw[0:1]
    y = y + delayed(2) * cw[1:2]
    y = y + delayed(1) * cw[2:3]
    y = y + x * cw[3:4]
    prev_sc[...] = x[tc - 8:tc]
    a, u = _lru_gates(y, wa_ref, ba_ref[...], wi_ref, bi_ref[...], lam_ref[...])
    row = lax.broadcasted_iota(jnp.int32, (tc, 1), 0)
    u = jnp.where(c * tc + row >= front, u, 0.0)
    a = a.reshape(tc // 8, 8, rw)
    u = u.reshape(tc // 8, 8, rw)
    in_group = lax.broadcasted_iota(jnp.int32, (1, 8, 1), 1)
    d = 1
    while d < 8:
        keep = in_group >= d
        u = jnp.where(keep, a * pltpu.roll(u, d, 1) + u, u)
        a = jnp.where(keep, a * pltpu.roll(a, d, 1), a)
        d *= 2
    carry = h_sc[0:1, :]
    groups = []
    for g in range(tc // 8):
        hg = a[g] * carry + u[g]
        groups.append(hg)
        carry = hg[7:8]
    h = jnp.concatenate(groups, axis=0)
    h_sc[0:1, :] = carry
    hl_ref[...] = groups[-1]
    rnn_ref[...] = (_gelu_tanh(xg_ref[...]) * h).astype(BF16)


def _rglru_prompt(xr_all, xg_all, cw, cb, wa_bd, ba, wi_bd, bi, lam, n_batch, lp, front):
    tc = ROW_TILE
    nc = lp // tc
    rw = xr_all.shape[1]
    kern = functools.partial(_rglru_prompt_body, tc=tc, front=front)
    row = lambda b, c: (b * nc + c, 0)
    const2 = lambda b, c: (0, 0)
    const3 = lambda b, c: (0, 0, 0)
    return pl.pallas_call(
        kern,
        grid=(n_batch, nc),
        in_specs=[
            pl.BlockSpec((tc, rw), row),
            pl.BlockSpec((tc, rw), row),
            pl.BlockSpec(cw.shape, const2),
            pl.BlockSpec(cb.shape, const2),
            pl.BlockSpec(wa_bd.shape, const3),
            pl.BlockSpec(ba.shape, const2),
            pl.BlockSpec(wi_bd.shape, const3),
            pl.BlockSpec(bi.shape, const2),
            pl.BlockSpec(lam.shape, const2),
        ],
        out_specs=[
            pl.BlockSpec((tc, rw), row),
            pl.BlockSpec((8, rw), lambda b, c: (b, 0)),
        ],
        out_shape=[
            jax.ShapeDtypeStruct((n_batch * lp, rw), BF16),
            jax.ShapeDtypeStruct((n_batch * 8, rw), F32),
        ],
        scratch_shapes=[pltpu.VMEM((8, rw), F32), pltpu.VMEM((8, rw), F32)],
        compiler_params=_params(("parallel", "arbitrary")),
        name="rglru_prompt",
    )(xr_all, xg_all, cw, cb, wa_bd, ba, wi_bd, bi, lam)


def _rglru_sample_body(xr_ref, xg_ref, cs_ref, h0_ref, cw_ref, cb_ref, wa_ref, ba_ref, wi_ref, bi_ref, lam_ref,
                       rnn_ref, hl_ref):
    n_t = xr_ref.shape[0]
    n_c = cs_ref.shape[0]
    rows = [cs_ref[i] for i in range(n_c)] + [xr_ref[t] for t in range(n_t)]
    cw = cw_ref[...]
    h = h0_ref[...]
    for t in range(n_t):
        y = cb_ref[...] + rows[t] * cw[0:1]
        for j in range(1, CONV_W):
            y = y + rows[t + j] * cw[j:j + 1]
        a, u = _lru_gates(y, wa_ref, ba_ref[...], wi_ref, bi_ref[...], lam_ref[...])
        h = a * h + u
        rnn_ref[t] = (_gelu_tanh(xg_ref[t]) * h).astype(BF16)
    hl_ref[...] = h


def _rglru_sample(xr_t, xg_t, conv_t, h0, cw, cb, wa_bd, ba, wi_bd, bi, lam):
    n_t, n_seq, rw = xr_t.shape
    return pl.pallas_call(
        _rglru_sample_body,
        out_shape=[
            jax.ShapeDtypeStruct((n_t, n_seq, rw), BF16),
            jax.ShapeDtypeStruct((n_seq, rw), F32),
        ],
        compiler_params=pltpu.CompilerParams(vmem_limit_bytes=VMEM_LIMIT),
        name="rglru_sample",
    )(xr_t, xg_t, conv_t, h0, cw, cb, wa_bd, ba, wi_bd, bi, lam)


def _route(logits):
    g = [logits[i:i + 1] for i in range(N_GROUPS)]
    gmax = functools.reduce(jnp.maximum, g)
    g_idx = jnp.full(gmax.shape, N_GROUPS - 1, jnp.int32)
    for i in range(N_GROUPS - 2, -1, -1):
        g_idx = jnp.where(g[i] == gmax, i, g_idx)
    g_sel = 1.0 / functools.reduce(jnp.add, [jnp.exp(gi - gmax) for gi in g])
    e = []
    for j in range(EXP_PER_GROUP):
        ej = logits[N_GROUPS + j:N_GROUPS + j + 1]
        for grp in range(1, N_GROUPS):
            r = N_GROUPS + grp * EXP_PER_GROUP + j
            ej = jnp.where(g_idx == grp, logits[r:r + 1], ej)
        e.append(ej)
    v1 = functools.reduce(jnp.maximum, e)
    i1 = jnp.full(v1.shape, EXP_PER_GROUP - 1, jnp.int32)
    for j in range(EXP_PER_GROUP - 2, -1, -1):
        i1 = jnp.where(e[j] == v1, j, i1)
    rest = [jnp.where(i1 == j, -jnp.inf, e[j]) for j in range(EXP_PER_GROUP)]
    v2 = functools.reduce(jnp.maximum, rest)
    i2 = jnp.full(v2.shape, EXP_PER_GROUP - 1, jnp.int32)
    for j in range(EXP_PER_GROUP - 2, -1, -1):
        i2 = jnp.where(rest[j] == v2, j, i2)
    z = jnp.exp(v2 - v1)
    w1 = g_sel / (1.0 + z)
    w2 = g_sel * z / (1.0 + z)
    e1 = g_idx * EXP_PER_GROUP + i1
    e2 = g_idx * EXP_PER_GROUP + i2
    rows = [jnp.where(e1 == x, w1, 0.0) + jnp.where(e2 == x, w2, 0.0) for x in range(N_EXPERTS)]
    return jnp.concatenate(rows, axis=0)


def _outproj_body(head_ref, xp_ref, xs_ref, attp_ref, atts_ref, rnnp_ref, rnns_ref, wo_ref, g2_ref, wr_ref, br_ref,
                  h1_ref, hn_ref, gates_ref, *, tiles):
    is_prompt = pl.program_id(0) < tiles.n_prompt
    aw = attp_ref.shape[1]
    att = jnp.where(is_prompt, attp_ref[...], atts_ref[...])
    rnn = jnp.where(is_prompt, rnnp_ref[...], rnns_ref[...])
    mix = jnp.dot(att, wo_ref[0:aw, :], preferred_element_type=F32)
    mix = mix + jnp.dot(rnn, wo_ref[aw:, :], preferred_element_type=F32)
    h1 = tiles.select(head_ref, xp_ref, xs_ref) + mix
    h1_ref[...] = h1
    ms = jnp.mean(h1 * h1, axis=-1, keepdims=True)
    hn = h1 * lax.rsqrt(ms + NORM_EPS) * g2_ref[...]
    hn_hi = hn.astype(BF16)
    hn_ref[...] = hn_hi
    hn_lo = (hn - hn_hi.astype(F32)).astype(BF16)
    lg = jnp.dot(hn_hi, wr_ref[0], preferred_element_type=F32)
    lg = lg + jnp.dot(hn_lo, wr_ref[0], preferred_element_type=F32)
    lg = lg + jnp.dot(hn_hi, wr_ref[1], preferred_element_type=F32)
    logits = jnp.transpose(lg)[0:br_ref.shape[0]] + br_ref[...]
    gates_ref[...] = _route(logits)


def _outproj(tiles, head, xp, xs, att_p, att_s, rnn_p, rnn_s, w_out_b, norm2, w_router, b_router):
    d = head.shape[1]
    tm = ROW_TILE
    nt = tiles.n_tiles * tm
    npt = tiles.n_prompt
    row = lambda i: (i, 0)
    const = lambda i: (0, 0)
    prow = lambda i: (jnp.minimum(i, npt - 1), 0)
    srow = lambda i: (tiles.sample_block(i), 0)
    kern = functools.partial(_outproj_body, tiles=tiles)
    return pl.pallas_call(
        kern,
        grid=(tiles.n_tiles,),
        in_specs=tiles.specs(d) + [
            pl.BlockSpec((tm, att_p.shape[1]), prow),
            pl.BlockSpec((tm, att_s.shape[1]), srow),
            pl.BlockSpec((tm, rnn_p.shape[1]), prow),
            pl.BlockSpec((tm, rnn_s.shape[1]), srow),
            pl.BlockSpec(w_out_b.shape, const),
            pl.BlockSpec(norm2.shape, const),
            pl.BlockSpec(w_router.shape, lambda i: (0, 0, 0)),
            pl.BlockSpec(b_router.shape, const),
        ],
        out_specs=[
            pl.BlockSpec((tm, d), row),
            pl.BlockSpec((tm, d), row),
            pl.BlockSpec((N_EXPERTS, tm), lambda i: (0, i)),
        ],
        out_shape=[
            jax.ShapeDtypeStruct((nt, d), F32),
            jax.ShapeDtypeStruct((nt, d), BF16),
            jax.ShapeDtypeStruct((N_EXPERTS, nt), F32),
        ],
        compiler_params=_params(("parallel",)),
        name="outproj_router",
    )(head, xp, xs, att_p, att_s, rnn_p, rnn_s, w_out_b, norm2, w_router, b_router)


def _moe_body(hn_ref, h1_ref, gates_ref, wg_ref, wu_ref, wd_ref, nf_ref, y_ref, acc_sc):
    e = pl.program_id(1)

    @pl.when(e == 0)
    def _():
        acc_sc[...] = jnp.zeros(acc_sc.shape, F32)

    x = hn_ref[...]
    hg = jnp.dot(x, wg_ref[...], preferred_element_type=F32)
    hu = jnp.dot(x, wu_ref[...], preferred_element_type=F32)
    he = (hg * _sigmoid(hg) * hu).astype(BF16)
    out = jnp.dot(he, wd_ref[...], preferred_element_type=F32)
    gates = gates_ref[...]
    lane = lax.broadcasted_iota(jnp.int32, gates.shape, 1)
    gcol = jnp.sum(jnp.where(lane == e, gates, 0.0), axis=1, keepdims=True)
    acc_sc[...] = acc_sc[...] + gcol * out

    @pl.when(e == pl.num_programs(1) - 1)
    def _():
        h2 = h1_ref[...] + acc_sc[...]
        ms = jnp.mean(h2 * h2, axis=-1, keepdims=True)
        y_ref[...] = h2 * lax.rsqrt(ms + NORM_EPS) * nf_ref[...]


def _moe(hn2, h1, gates, wg_b, wu_b, wd_b, norm_f):
    nt, d = h1.shape
    de = wg_b.shape[2]
    tm = 512 if nt % 512 == 0 else ROW_TILE
    row = lambda i, e: (i, 0)
    return pl.pallas_call(
        _moe_body,
        grid=(nt // tm, N_EXPERTS),
        in_specs=[
            pl.BlockSpec((tm, d), row),
            pl.BlockSpec((tm, d), row),
            pl.BlockSpec((tm, N_EXPERTS), row),
            pl.BlockSpec((None, d, de), lambda i, e: (e, 0, 0)),
            pl.BlockSpec((None, d, de), lambda i, e: (e, 0, 0)),
            pl.BlockSpec((None, de, d), lambda i, e: (e, 0, 0)),
            pl.BlockSpec((1, d), lambda i, e: (0, 0)),
        ],
        out_specs=pl.BlockSpec((tm, d), row),
        out_shape=jax.ShapeDtypeStruct((nt, d), F32),
        scratch_shapes=[pltpu.VMEM((tm, d), F32)],
        compiler_params=_params(("parallel", "arbitrary")),
        name="moe_dense",
    )(hn2, h1, gates, wg_b, wu_b, wd_b, norm_f)


def _rope_tables(pos):
    half = ROPE_DIM // 2
    inv = jnp.power(ROPE_THETA, -2.0 * jnp.arange(half, dtype=F32) / ROPE_DIM)
    ang = pos[:, None] * inv[None, :]
    cos, sin = jnp.cos(ang), jnp.sin(ang)
    n = pos.shape[0]
    pad = jnp.zeros((n, HEAD_QK - ROPE_DIM), F32)
    ct = jnp.concatenate([cos, cos, pad + 1.0], axis=1)
    s1 = jnp.concatenate([-sin, jnp.zeros_like(sin), pad], axis=1)
    s2 = jnp.concatenate([jnp.zeros_like(sin), sin, pad], axis=1)
    rep = LANES // HEAD_QK
    return jnp.stack([jnp.tile(t, (1, rep)) for t in (ct, s1, s2)])


def _block_diag_halves(w):
    nb, bs, _ = w.shape
    hb = nb // 2
    eye = jnp.eye(hb, dtype=w.dtype)
    halves = [jnp.einsum('nkj,nm->nkmj', w[i * hb:(i + 1) * hb], eye).reshape(hb * bs, hb * bs) for i in range(2)]
    return jnp.stack(halves).astype(BF16)


def kernel(x_prompt, x_sample, cache_k, cache_v, state_conv, state_h, page_table, meta_tokens, norm1, w_in,
           lambda_q1, lambda_k1, lambda_q2, lambda_k2, subln, conv_w, conv_b, rg_w_a, rg_b_a, rg_w_i, rg_b_i,
           lru_lambda, w_out, norm2, w_grp, b_grp, w_rt, b_rt, w_gate, w_up, w_down, norm_f):
    assert w_in.shape[0] == 1, "single-layer trunk only"
    n_batch, seq, d = x_prompt.shape
    n_seq, n_t, _ = x_sample.shape
    assert seq % ROW_TILE == 0 and n_t >= CONV_W - 1
    front = ROW_TILE - N_META
    lp = ROW_TILE + seq
    n_pages, page = page_table.shape[1], cache_k.shape[2]
    past_len = n_pages * page
    ns = n_seq * n_t
    ns_pad = -(-ns // ROW_TILE) * ROW_TILE
    np_rows = n_batch * lp
    rw = lru_lambda.shape[1]

    assert ROW_TILE % n_t == 0
    tiles = _TokenTiles(n_batch, lp // ROW_TILE, ns_pad // ROW_TILE)
    head = jnp.concatenate([jnp.zeros((front, d), F32), meta_tokens.astype(F32)], axis=0)
    xp = x_prompt.reshape(n_batch * seq, d)
    xs = jnp.pad(x_sample.reshape(ns, d), ((0, ns_pad - ns), (0, 0)))
    pos_p = jnp.maximum(jnp.arange(lp) - front, 0)
    pos_s = past_len + jnp.arange(ROW_TILE) % n_t
    cs = _rope_tables(jnp.concatenate([pos_p, pos_s]).astype(F32))

    lamv = jnp.stack([lambda_q1[0], lambda_k1[0], lambda_q2[0], lambda_k2[0]]).astype(F32)
    gain = subln.astype(F32)

    q_all, k_all, kb_all, v_all, vb_all, xr_all, xg_all = _inproj(tiles, head, xp, xs, norm1,
                                                                  w_in[0].astype(BF16), cs)

    att_p = _prompt_attention(q_all, kb_all, vb_all, lamv, gain, n_batch, lp, front)
    wa_bd, wi_bd = _block_diag_halves(rg_w_a[0]), _block_diag_halves(rg_w_i[0])
    rnn_w = (conv_w[0], conv_b, wa_bd, rg_b_a, wi_bd, rg_b_i, lru_lambda)
    rnn_p, hl_p = _rglru_prompt(xr_all, xg_all, *rnn_w, n_batch, lp, front)

    sl = slice(np_rows, np_rows + ns)
    q_s = q_all[sl].reshape(n_seq, n_t, ATT_HEADS * 2, HEAD_QK)
    eye = jnp.eye(ATT_HEADS * 2, dtype=BF16)
    qbd = jnp.einsum('bthd,hg->bhtgd', q_s, eye).reshape(n_seq, ATT_HEADS * 2 * n_t, Q_WIDTH)
    att_s = _sample_attention(qbd, k_all[sl].reshape(n_seq, n_t, Q_WIDTH), v_all[sl].reshape(n_seq, n_t, ATT_WIDTH),
                              jnp.transpose(cache_k[0], (0, 2, 3, 4, 1)).reshape(-1, Q_WIDTH, page),
                              cache_v[0].reshape(-1, page * ATT_HEADS, HEAD_V),
                              page_table, lamv, gain)
    att_s = jnp.concatenate([att_s.reshape(ns, ATT_WIDTH).astype(BF16),
                             jnp.zeros((ns_pad - ns, ATT_WIDTH), BF16)], axis=0)
    to_tmajor = lambda a: jnp.swapaxes(a[sl].reshape(n_seq, n_t, rw), 0, 1)
    rnn_s, hl_s = _rglru_sample(to_tmajor(xr_all), to_tmajor(xg_all), jnp.swapaxes(state_conv[0], 0, 1),
                                state_h[0].astype(F32), *rnn_w)
    rnn_s = jnp.concatenate([jnp.swapaxes(rnn_s, 0, 1).reshape(ns, rw), jnp.zeros((ns_pad - ns, rw), BF16)], axis=0)

    w_r32 = jnp.concatenate([w_grp[0], w_rt[0], jnp.zeros((d, LANES - N_GROUPS - N_EXPERTS), F32)],
                            axis=1).astype(F32)
    w_r_hi = w_r32.astype(BF16)
    w_router = jnp.stack([w_r_hi, (w_r32 - w_r_hi.astype(F32)).astype(BF16)])
    b_router = jnp.concatenate([b_grp[0], b_rt[0], jnp.zeros((32 - N_GROUPS - N_EXPERTS,), F32)])[:, None]
    h1, hn2, gates_t = _outproj(tiles, head, xp, xs, att_p, att_s, rnn_p, rnn_s, w_out[0].astype(BF16), norm2,
                                w_router, b_router.astype(F32))
    y_all = _moe(hn2, h1, gates_t.T, w_gate[0].astype(BF16), w_up[0].astype(BF16), w_down[0].astype(BF16),
                 norm_f[None, :])

    def prompt_rows(a, skip):
        return jnp.stack([a[b * lp + skip:(b + 1) * lp] for b in range(n_batch)])

    y_prompt = prompt_rows(y_all, ROW_TILE)
    y_sample = y_all[sl].reshape(n_seq, n_t, d)
    k_prompt = prompt_rows(k_all, front).reshape(1, n_batch, N_META + seq, ATT_HEADS, 2, HEAD_QK)
    v_prompt = prompt_rows(v_all, front).reshape(1, n_batch, N_META + seq, ATT_HEADS, HEAD_V)
    k_sample = k_all[sl].reshape(1, n_seq, n_t, ATT_HEADS, 2, HEAD_QK)
    v_sample = v_all[sl].reshape(1, n_seq, n_t, ATT_HEADS, HEAD_V)
    conv_prompt = prompt_rows(xr_all, lp - (CONV_W - 1))[None]
    conv_sample = xr_all[sl].reshape(n_seq, n_t, rw)[:, n_t - (CONV_W - 1):][None]
    h_prompt = hl_p.reshape(n_batch, 8, rw)[:, 7][None]
    h_sample = hl_s[None]
    return (y_prompt, y_sample, k_prompt, v_prompt, k_sample, v_sample, conv_prompt, conv_sample,
            h_prompt, h_sample)
```

```python
import functools
import math

import jax
import jax.numpy as jnp
from jax import lax
from jax.experimental import pallas as pl
from jax.experimental.pallas import tpu as pltpu

F32 = jnp.float32
BF16 = jnp.bfloat16

N_META = 16
ATT_HEADS = 4
HEAD_QK = 64
HEAD_V = 2 * HEAD_QK
ATT_WIDTH = ATT_HEADS * HEAD_V
Q_WIDTH = ATT_HEADS * 2 * HEAD_QK
ROPE_DIM = HEAD_QK // 4
ROPE_THETA = 500000.0
RNN_BLOCKS = 8
CONV_W = 4
LRU_C = 8.0
N_GROUPS = 4
EXP_PER_GROUP = 4
N_EXPERTS = N_GROUPS * EXP_PER_GROUP
NORM_EPS = 1e-6
NEG_INF = -1e30
LAM_INIT = 0.8 - 0.6 * math.exp(-0.3 * 0)

LANES = 128
ROW_TILE = 256
VMEM_LIMIT = 56 * 1024 * 1024


def _params(sem, vmem=VMEM_LIMIT):
    return pltpu.CompilerParams(dimension_semantics=sem, vmem_limit_bytes=vmem)


def _sigmoid(x):
    return 1.0 / (1.0 + jnp.exp(-x))


def _softplus(x):
    return jnp.maximum(x, 0.0) + jnp.log1p(jnp.exp(-jnp.abs(x)))


def _gelu_tanh(x):
    return 0.5 * x * (1.0 + jnp.tanh(math.sqrt(2.0 / math.pi) * (x + 0.044715 * (x * x * x))))


def _lam_scalar(lamv):
    s1 = jnp.sum(lamv[0:1] * lamv[1:2], axis=-1, keepdims=True)
    s2 = jnp.sum(lamv[2:3] * lamv[3:4], axis=-1, keepdims=True)
    return jnp.exp(s1) - jnp.exp(s2) + LAM_INIT


class _TokenTiles:
    def __init__(self, n_batch, tiles_per_batch, n_sample_tiles):
        self.tpb = tiles_per_batch
        self.n_prompt = n_batch * tiles_per_batch
        self.n_sample = n_sample_tiles
        self.n_tiles = self.n_prompt + n_sample_tiles

    def prompt_block(self, i):
        blk = (i // self.tpb) * (self.tpb - 1) + jnp.maximum(i % self.tpb - 1, 0)
        return jnp.minimum(blk, self.n_prompt - self.n_prompt // self.tpb - 1)

    def sample_block(self, i):
        return jnp.clip(i - self.n_prompt, 0, self.n_sample - 1)

    def specs(self, d):
        tm = ROW_TILE
        return [pl.BlockSpec((tm, d), lambda i: (0, 0)),
                pl.BlockSpec((tm, d), lambda i: (self.prompt_block(i), 0)),
                pl.BlockSpec((tm, d), lambda i: (self.sample_block(i), 0))]

    def select(self, head_ref, xp_ref, xs_ref):
        i = pl.program_id(0)
        x = jnp.where(i % self.tpb == 0, head_ref[...], xp_ref[...])
        return jnp.where(i >= self.n_prompt, xs_ref[...], x)


def _inproj_body(head_ref, xp_ref, xs_ref, g_ref, w_ref, cs_ref, q_ref, kp_ref, ks_ref, kb_ref, vp_ref, vs_ref, vb_ref,
                 xr_ref, xg_ref, *, tiles):
    x = tiles.select(head_ref, xp_ref, xs_ref)
    is_prompt = pl.program_id(0) < tiles.n_prompt
    ms = jnp.mean(x * x, axis=-1, keepdims=True)
    hn = (x * lax.rsqrt(ms + NORM_EPS) * g_ref[...]).astype(BF16)
    proj = jnp.dot(hn, w_ref[...], preferred_element_type=F32)
    ct, s1, s2 = cs_ref[0], cs_ref[1], cs_ref[2]

    def rope(xc):
        return xc * ct + pltpu.roll(xc, LANES - ROPE_DIM // 2, 1) * s1 + pltpu.roll(xc, ROPE_DIM // 2, 1) * s2

    k_chunks = []
    for c in range(Q_WIDTH // LANES):
        sl = slice(c * LANES, (c + 1) * LANES)
        qc = rope(proj[:, c * LANES:(c + 1) * LANES])
        q_ref[:, sl] = (qc * (HEAD_QK ** -0.5)).astype(BF16)
        kc = rope(proj[:, Q_WIDTH + c * LANES:Q_WIDTH + (c + 1) * LANES])
        k_chunks.append(kc)
        kb_ref[:, sl] = kc.astype(BF16)
    k = jnp.concatenate(k_chunks, axis=1)
    v = proj[:, 2 * Q_WIDTH:2 * Q_WIDTH + ATT_WIDTH]
    vb_ref[...] = v.astype(BF16)

    @pl.when(is_prompt)
    def _():
        kp_ref[...] = k
        vp_ref[...] = v

    @pl.when(jnp.logical_not(is_prompt))
    def _():
        ks_ref[...] = k
        vs_ref[...] = v

    r0 = 2 * Q_WIDTH + ATT_WIDTH
    rw = xr_ref.shape[1]
    xr_ref[...] = proj[:, r0:r0 + rw]
    xg_ref[...] = proj[:, r0 + rw:r0 + 2 * rw]


def _inproj(tiles, head, xp, xs, norm1, w_in_b, cs):
    d = head.shape[1]
    tm = ROW_TILE
    nt = tiles.n_tiles * tm
    rw = (w_in_b.shape[1] - 2 * Q_WIDTH - ATT_WIDTH) // 2
    row = lambda i: (i, 0)
    const = lambda i: (0, 0)
    cs_tile = lambda i: (0, jnp.where(i < tiles.n_prompt, i % tiles.tpb, tiles.tpb), 0)
    npr, nsr = tiles.n_prompt * tm, tiles.n_sample * tm
    prow = lambda i: (jnp.minimum(i, tiles.n_prompt - 1), 0)
    srow = lambda i: (tiles.sample_block(i), 0)
    outs = [
        ((nt, Q_WIDTH), BF16, row),
        ((npr, Q_WIDTH), F32, prow),
        ((nsr, Q_WIDTH), F32, srow),
        ((nt, Q_WIDTH), BF16, row),
        ((npr, ATT_WIDTH), F32, prow),
        ((nsr, ATT_WIDTH), F32, srow),
        ((nt, ATT_WIDTH), BF16, row),
        ((nt, rw), F32, row),
        ((nt, rw), F32, row),
    ]
    return pl.pallas_call(
        functools.partial(_inproj_body, tiles=tiles),
        grid=(tiles.n_tiles,),
        in_specs=tiles.specs(d) + [
            pl.BlockSpec((1, d), const),
            pl.BlockSpec(w_in_b.shape, const),
            pl.BlockSpec((3, tm, LANES), cs_tile),
        ],
        out_specs=[pl.BlockSpec((tm, s[1]), m) for s, _, m in outs],
        out_shape=[jax.ShapeDtypeStruct(s, t) for s, t, _ in outs],
        compiler_params=_params(("arbitrary",)),
        name="inproj",
    )(head, xp, xs, norm1, w_in_b, cs)


def _flash_body(q_ref, k_ref, v_ref, lamv_ref, gain_ref, o_ref, q2_sc, m_sc, acc_sc, *, tq, front):
    qi = pl.program_id(1)
    lane = lax.broadcasted_iota(jnp.int32, (tq, LANES), 1)
    for h in range(ATT_HEADS):
        q = q_ref[:, h * LANES:(h + 1) * LANES]
        zero = jnp.zeros_like(q)
        q2_sc[h] = jnp.concatenate([jnp.where(lane < HEAD_QK, q, zero), jnp.where(lane >= HEAD_QK, q, zero)], axis=0)
    m_sc[...] = jnp.full(m_sc.shape, NEG_INF, F32)
    acc_sc[...] = jnp.zeros(acc_sc.shape, F32)

    def step(ki, width, masked):
        start = pl.multiple_of(ki * tq, tq)
        n_ch = width // LANES
        ones = jnp.ones((width, LANES), BF16)
        for h in range(ATT_HEADS):
            k = k_ref[pl.ds(start, width), h * LANES:(h + 1) * LANES]
            v = v_ref[pl.ds(start, width), h * LANES:(h + 1) * LANES]
            s = lax.dot_general(q2_sc[h], k, (((1,), (1,)), ((), ())), preferred_element_type=F32)
            if masked:
                row = lax.broadcasted_iota(jnp.int32, s.shape, 0)
                col = lax.broadcasted_iota(jnp.int32, s.shape, 1)
                qpos = qi * tq + jnp.where(row >= tq, row - tq, row)
                kpos = ki * tq + col
                s = jnp.where(kpos <= qpos, jnp.where(kpos >= front, s, NEG_INF), NEG_INF)
            chunks = [s[:, c * LANES:(c + 1) * LANES] for c in range(n_ch)]
            m_prev = m_sc[h]
            m_cur = jnp.max(functools.reduce(jnp.maximum, chunks), axis=1, keepdims=True)
            m_next = jnp.maximum(m_prev, m_cur)
            p = jnp.concatenate([jnp.exp(c - m_next) for c in chunks], axis=1).astype(BF16)
            alpha = jnp.exp(m_prev - m_next)
            pv = jnp.dot(p, jnp.concatenate([v, ones], axis=1), preferred_element_type=F32)
            acc_sc[h] = jnp.concatenate([alpha, alpha], axis=1) * acc_sc[h] + pv
            m_sc[h] = m_next

    step(0, tq, True)
    n_plain = jnp.maximum(qi - 1, 0)

    def body(j, carry):
        step(1 + 2 * j, 2 * tq, False)
        return carry

    lax.fori_loop(0, n_plain // 2, body, 0)

    @pl.when(n_plain % 2 == 1)
    def _():
        step(qi - 1, tq, False)

    @pl.when(qi > 0)
    def _():
        step(qi, tq, True)

    lam = _lam_scalar(lamv_ref[...])
    for h in range(ATT_HEADS):
        acc = acc_sc[h]
        o = acc[:, :HEAD_V] / acc[:, HEAD_V:]
        d = o[:tq] - lam * o[tq:]
        ms = jnp.mean(d * d, axis=-1, keepdims=True)
        o_ref[:, h * HEAD_V:(h + 1) * HEAD_V] = (
            d * lax.rsqrt(ms + NORM_EPS) * gain_ref[...] * (1.0 - LAM_INIT)).astype(BF16)


def _prompt_attention(q_all, kb_all, vb_all, lamv, gain, n_batch, lp, front):
    tq = ROW_TILE
    nq = lp // tq
    kern = functools.partial(_flash_body, tq=tq, front=front)
    return pl.pallas_call(
        kern,
        grid=(n_batch, nq),
        in_specs=[
            pl.BlockSpec((tq, Q_WIDTH), lambda b, i: (b * nq + i, 0)),
            pl.BlockSpec((lp, Q_WIDTH), lambda b, i: (b, 0)),
            pl.BlockSpec((lp, ATT_WIDTH), lambda b, i: (b, 0)),
            pl.BlockSpec(lamv.shape, lambda b, i: (0, 0)),
            pl.BlockSpec(gain.shape, lambda b, i: (0, 0)),
        ],
        out_specs=pl.BlockSpec((tq, ATT_WIDTH), lambda b, i: (b * nq + i, 0)),
        out_shape=jax.ShapeDtypeStruct((n_batch * lp, ATT_WIDTH), BF16),
        scratch_shapes=[
            pltpu.VMEM((ATT_HEADS, 2 * tq, LANES), BF16),
            pltpu.VMEM((ATT_HEADS, 2 * tq, LANES), F32),
            pltpu.VMEM((ATT_HEADS, 2 * tq, 2 * HEAD_V), F32),
        ],
        compiler_params=_params(("parallel", "arbitrary")),
        name="prompt_attn",
    )(q_all, kb_all, vb_all, lamv, gain)


def _paged_body(pt_ref, qbd_ref, kn_ref, vn_ref, lamv_ref, gain_ref, *rest, n_pg, n_chunks, n_t):
    del pt_ref
    k_refs = rest[:n_pg]
    v_refs = rest[n_pg:2 * n_pg]
    o_ref = rest[2 * n_pg]
    m_sc, l_sc, acc_sc = rest[2 * n_pg + 1:]
    j = pl.program_id(1)
    n_rows = qbd_ref.shape[0]

    @pl.when(j == 0)
    def _():
        m_sc[...] = jnp.full(m_sc.shape, NEG_INF, F32)
        l_sc[...] = jnp.zeros(l_sc.shape, F32)
        acc_sc[...] = jnp.zeros(acc_sc.shape, F32)

    qbd = qbd_ref[...]
    chunks = [jnp.dot(qbd, k_refs[i][...].astype(BF16), preferred_element_type=F32) for i in range(n_pg)]
    m_prev = m_sc[...]
    m_cur = jnp.max(functools.reduce(jnp.maximum, chunks), axis=1, keepdims=True)
    m_next = jnp.maximum(m_prev, m_cur)
    p = [jnp.exp(c - m_next) for c in chunks]
    alpha = jnp.exp(m_prev - m_next)
    l_sc[...] = alpha * l_sc[...] + jnp.sum(functools.reduce(jnp.add, p), axis=1, keepdims=True)
    page = v_refs[0].shape[0] // ATT_HEADS
    hr = 2 * n_t
    pv = []
    for h in range(ATT_HEADS):
        d = None
        for i in range(n_pg):
            vh = v_refs[i][pl.ds(h, page, stride=ATT_HEADS), :].astype(BF16)
            t = jnp.dot(p[i][h * hr:(h + 1) * hr].astype(BF16), vh, preferred_element_type=F32)
            d = t if d is None else d + t
        pv.append(d)
    acc_sc[...] = acc_sc[...] * alpha + jnp.concatenate(pv, axis=0)
    m_sc[...] = m_next

    @pl.when(j == n_chunks - 1)
    def _():
        qf = qbd.astype(F32)
        kn = kn_ref[...].astype(BF16).astype(F32)
        vn = vn_ref[...].astype(BF16).astype(F32)
        row_t = lax.broadcasted_iota(jnp.int32, (n_rows, 1), 0) % n_t
        s_new = []
        for t2 in range(n_t):
            st = jnp.sum(qf * kn[t2:t2 + 1, :], axis=1, keepdims=True)
            s_new.append(jnp.where(row_t >= t2, st, NEG_INF))
        m_old = m_sc[...]
        m_fin = m_old
        for st in s_new:
            m_fin = jnp.maximum(m_fin, st)
        a_fin = jnp.exp(m_old - m_fin)
        l_fin = a_fin * l_sc[...]
        acc = acc_sc[...] * a_fin
        for t2 in range(n_t):
            pt = jnp.exp(s_new[t2] - m_fin)
            l_fin = l_fin + pt
            v_rows = jnp.concatenate(
                [jnp.broadcast_to(vn[t2:t2 + 1, h * HEAD_V:(h + 1) * HEAD_V], (2 * n_t, HEAD_V))
                 for h in range(ATT_HEADS)], axis=0)
            acc = acc + pt * v_rows
        o = acc / l_fin
        lam = _lam_scalar(lamv_ref[...])
        for h in range(ATT_HEADS):
            r0 = h * 2 * n_t
            d = o[r0:r0 + n_t] - lam * o[r0 + n_t:r0 + 2 * n_t]
            ms = jnp.mean(d * d, axis=-1, keepdims=True)
            o_ref[:, h * HEAD_V:(h + 1) * HEAD_V] = d * lax.rsqrt(ms + NORM_EPS) * gain_ref[...] * (1.0 - LAM_INIT)


def _sample_attention(qbd, k_new, v_new, cache_kt, cache_vm, page_table, lamv, gain):
    n_seq, n_rows, kw = qbd.shape
    n_t = k_new.shape[1]
    n_pages = page_table.shape[1]
    n_pg = math.gcd(n_pages, 16)
    n_chunks = n_pages // n_pg
    pt_flat = page_table.reshape(-1)

    def page_map(i):
        return lambda b, j, pt: (pt[b * n_pages + j * n_pg + i], 0, 0)

    seq3 = lambda b, j, pt: (b, 0, 0)
    const2 = lambda b, j, pt: (0, 0)
    kern = functools.partial(_paged_body, n_pg=n_pg, n_chunks=n_chunks, n_t=n_t)
    grid_spec = pltpu.PrefetchScalarGridSpec(
        num_scalar_prefetch=1,
        grid=(n_seq, n_chunks),
        in_specs=[
            pl.BlockSpec((None, n_rows, kw), seq3),
            pl.BlockSpec((None, n_t, kw), seq3),
            pl.BlockSpec((None, n_t, kw), seq3),
            pl.BlockSpec(lamv.shape, const2),
            pl.BlockSpec(gain.shape, const2),
        ] + [pl.BlockSpec((None,) + cache_kt.shape[1:], page_map(i)) for i in range(n_pg)]
          + [pl.BlockSpec((None,) + cache_vm.shape[1:], page_map(i)) for i in range(n_pg)],
        out_specs=pl.BlockSpec((None, n_t, ATT_WIDTH), seq3),
        scratch_shapes=[
            pltpu.VMEM((n_rows, LANES), F32),
            pltpu.VMEM((n_rows, LANES), F32),
            pltpu.VMEM((n_rows, HEAD_V), F32),
        ],
    )
    return pl.pallas_call(
        kern,
        grid_spec=grid_spec,
        out_shape=jax.ShapeDtypeStruct((n_seq, n_t, ATT_WIDTH), F32),
        compiler_params=_params(("parallel", "arbitrary")),
        name="sample_attn",
    )(pt_flat, qbd, k_new, v_new, lamv, gain, *([cache_kt] * n_pg), *([cache_vm] * n_pg))


def _lru_gates(xc, wa_ref, ba, wi_ref, bi, lam_row):
    half = xc.shape[1] // 2
    xb = xc.astype(BF16)

    def blockdiag(w_ref):
        return jnp.concatenate([
            jnp.dot(xb[:, :half], w_ref[0], preferred_element_type=F32),
            jnp.dot(xb[:, half:], w_ref[1], preferred_element_type=F32)], axis=1)

    gate_r = _sigmoid(blockdiag(wa_ref) + ba)
    gate_i = _sigmoid(blockdiag(wi_ref) + bi)
    log_a = -LRU_C * gate_r * _softplus(-lam_row)
    a = jnp.exp(log_a)
    th = jnp.tanh(log_a)
    om = -2.0 * th / (1.0 - th)
    root = om * lax.rsqrt(jnp.maximum(om, jnp.finfo(jnp.float32).tiny))
    u = root * (gate_i * xc)
    return a, u


def _rglru_prompt_body(xr_ref, xg_ref, cw_ref, cb_ref, wa_ref, ba_ref, wi_ref, bi_ref, lam_ref,
                       rnn_ref, hl_ref, prev_sc, h_sc, *, tc, front):
    c = pl.program_id(1)

    @pl.when(c == 0)
    def _():
        prev_sc[...] = jnp.zeros(prev_sc.shape, F32)
        h_sc[...] = jnp.zeros(h_sc.shape, F32)

    x = xr_ref[...]
    prev = prev_sc[...]
    rw = x.shape[1]
    row8 = lax.broadcasted_iota(jnp.int32, (8, 1), 0)

    def delayed(d):
        r = pltpu.roll(x, d, 0)
        top = jnp.where(row8 < d, pltpu.roll(prev, d, 0), r[0:8])
        return jnp.concatenate([top, r[8:]], axis=0)

    cw = cw_ref[...]
    y = cb_ref[...] + delayed(3) * cw[0:1]
    y = y + delayed(2) * cw[1:2]
    y = y + delayed(1) * cw[2:3]
    y = y + x * cw[3:4]
    prev_sc[...] = x[tc - 8:tc]
    a, u = _lru_gates(y, wa_ref, ba_ref[...], wi_ref, bi_ref[...], lam_ref[...])
    row = lax.broadcasted_iota(jnp.int32, (tc, 1), 0)
    u = jnp.where(c * tc + row >= front, u, 0.0)
    a = a.reshape(tc // 8, 8, rw)
    u = u.reshape(tc // 8, 8, rw)
    in_group = lax.broadcasted_iota(jnp.int32, (1, 8, 1), 1)
    d = 1
    while d < 8:
        keep = in_group >= d
        u = jnp.where(keep, a * pltpu.roll(u, d, 1) + u, u)
        a = jnp.where(keep, a * pltpu.roll(a, d, 1), a)
        d *= 2
    carry = h_sc[0:1, :]
    groups = []
    for g in range(tc // 8):
        hg = a[g] * carry + u[g]
        groups.append(hg)
        carry = hg[7:8]
    h = jnp.concatenate(groups, axis=0)
    h_sc[0:1, :] = carry
    hl_ref[...] = groups[-1]
    rnn_ref[...] = (_gelu_tanh(xg_ref[...]) * h).astype(BF16)


def _rglru_prompt(xr_all, xg_all, cw, cb, wa_bd, ba, wi_bd, bi, lam, n_batch, lp, front):
    tc = ROW_TILE
    nc = lp // tc
    rw = xr_all.shape[1]
    kern = functools.partial(_rglru_prompt_body, tc=tc, front=front)
    row = lambda b, c: (b * nc + c, 0)
    const2 = lambda b, c: (0, 0)
    const3 = lambda b, c: (0, 0, 0)
    return pl.pallas_call(
        kern,
        grid=(n_batch, nc),
        in_specs=[
            pl.BlockSpec((tc, rw), row),
            pl.BlockSpec((tc, rw), row),
            pl.BlockSpec(cw.shape, const2),
            pl.BlockSpec(cb.shape, const2),
            pl.BlockSpec(wa_bd.shape, const3),
            pl.BlockSpec(ba.shape, const2),
            pl.BlockSpec(wi_bd.shape, const3),
            pl.BlockSpec(bi.shape, const2),
            pl.BlockSpec(lam.shape, const2),
        ],
        out_specs=[
            pl.BlockSpec((tc, rw), row),
            pl.BlockSpec((8, rw), lambda b, c: (b, 0)),
        ],
        out_shape=[
            jax.ShapeDtypeStruct((n_batch * lp, rw), BF16),
            jax.ShapeDtypeStruct((n_batch * 8, rw), F32),
        ],
        scratch_shapes=[pltpu.VMEM((8, rw), F32), pltpu.VMEM((8, rw), F32)],
        compiler_params=_params(("parallel", "arbitrary")),
        name="rglru_prompt",
    )(xr_all, xg_all, cw, cb, wa_bd, ba, wi_bd, bi, lam)


def _rglru_sample_body(xr_ref, xg_ref, cs_ref, h0_ref, cw_ref, cb_ref, wa_ref, ba_ref, wi_ref, bi_ref, lam_ref,
                       rnn_ref, hl_ref):
    n_t = xr_ref.shape[0]
    n_c = cs_ref.shape[0]
    rows = [cs_ref[i] for i in range(n_c)] + [xr_ref[t] for t in range(n_t)]
    cw = cw_ref[...]
    h = h0_ref[...]
    for t in range(n_t):
        y = cb_ref[...] + rows[t] * cw[0:1]
        for j in range(1, CONV_W):
            y = y + rows[t + j] * cw[j:j + 1]
        a, u = _lru_gates(y, wa_ref, ba_ref[...], wi_ref, bi_ref[...], lam_ref[...])
        h = a * h + u
        rnn_ref[t] = (_gelu_tanh(xg_ref[t]) * h).astype(BF16)
    hl_ref[...] = h


def _rglru_sample(xr_t, xg_t, conv_t, h0, cw, cb, wa_bd, ba, wi_bd, bi, lam):
    n_t, n_seq, rw = xr_t.shape
    return pl.pallas_call(
        _rglru_sample_body,
        out_shape=[
            jax.ShapeDtypeStruct((n_t, n_seq, rw), BF16),
            jax.ShapeDtypeStruct((n_seq, rw), F32),
        ],
        compiler_params=pltpu.CompilerParams(vmem_limit_bytes=VMEM_LIMIT),
        name="rglru_sample",
    )(xr_t, xg_t, conv_t, h0, cw, cb, wa_bd, ba, wi_bd, bi, lam)


def _route(logits):
    g = [logits[i:i + 1] for i in range(N_GROUPS)]
    gmax = functools.reduce(jnp.maximum, g)
    g_idx = jnp.full(gmax.shape, N_GROUPS - 1, jnp.int32)
    for i in range(N_GROUPS - 2, -1, -1):
        g_idx = jnp.where(g[i] == gmax, i, g_idx)
    g_sel = 1.0 / functools.reduce(jnp.add, [jnp.exp(gi - gmax) for gi in g])
    e = []
    for j in range(EXP_PER_GROUP):
        ej = logits[N_GROUPS + j:N_GROUPS + j + 1]
        for grp in range(1, N_GROUPS):
            r = N_GROUPS + grp * EXP_PER_GROUP + j
            ej = jnp.where(g_idx == grp, logits[r:r + 1], ej)
        e.append(ej)
    v1 = functools.reduce(jnp.maximum, e)
    i1 = jnp.full(v1.shape, EXP_PER_GROUP - 1, jnp.int32)
    for j in range(EXP_PER_GROUP - 2, -1, -1):
        i1 = jnp.where(e[j] == v1, j, i1)
    rest = [jnp.where(i1 == j, -jnp.inf, e[j]) for j in range(EXP_PER_GROUP)]
    v2 = functools.reduce(jnp.maximum, rest)
    i2 = jnp.full(v2.shape, EXP_PER_GROUP - 1, jnp.int32)
    for j in range(EXP_PER_GROUP - 2, -1, -1):
        i2 = jnp.where(rest[j] == v2, j, i2)
    z = jnp.exp(v2 - v1)
    w1 = g_sel / (1.0 + z)
    w2 = g_sel * z / (1.0 + z)
    e1 = g_idx * EXP_PER_GROUP + i1
    e2 = g_idx * EXP_PER_GROUP + i2
    rows = [jnp.where(e1 == x, w1, 0.0) + jnp.where(e2 == x, w2, 0.0) for x in range(N_EXPERTS)]
    return jnp.concatenate(rows, axis=0)


def _outproj_body(head_ref, xp_ref, xs_ref, attp_ref, atts_ref, rnnp_ref, rnns_ref, wo_ref, g2_ref, wr_ref, br_ref,
                  h1_ref, hn_ref, gates_ref, *, tiles):
    is_prompt = pl.program_id(0) < tiles.n_prompt
    aw = attp_ref.shape[1]
    att = jnp.where(is_prompt, attp_ref[...], atts_ref[...])
    rnn = jnp.where(is_prompt, rnnp_ref[...], rnns_ref[...])
    mix = jnp.dot(att, wo_ref[0:aw, :], preferred_element_type=F32)
    mix = mix + jnp.dot(rnn, wo_ref[aw:, :], preferred_element_type=F32)
    h1 = tiles.select(head_ref, xp_ref, xs_ref) + mix
    h1_ref[...] = h1
    ms = jnp.mean(h1 * h1, axis=-1, keepdims=True)
    hn = h1 * lax.rsqrt(ms + NORM_EPS) * g2_ref[...]
    hn_hi = hn.astype(BF16)
    hn_ref[...] = hn_hi
    hn_lo = (hn - hn_hi.astype(F32)).astype(BF16)
    lg = jnp.dot(hn_hi, wr_ref[0], preferred_element_type=F32)
    lg = lg + jnp.dot(hn_lo, wr_ref[0], preferred_element_type=F32)
    lg = lg + jnp.dot(hn_hi, wr_ref[1], preferred_element_type=F32)
    logits = jnp.transpose(lg)[0:br_ref.shape[0]] + br_ref[...]
    gates_ref[...] = _route(logits)


def _outproj(tiles, head, xp, xs, att_p, att_s, rnn_p, rnn_s, w_out_b, norm2, w_router, b_router):
    d = head.shape[1]
    tm = ROW_TILE
    nt = tiles.n_tiles * tm
    npt = tiles.n_prompt
    row = lambda i: (i, 0)
    const = lambda i: (0, 0)
    prow = lambda i: (jnp.minimum(i, npt - 1), 0)
    srow = lambda i: (tiles.sample_block(i), 0)
    kern = functools.partial(_outproj_body, tiles=tiles)
    return pl.pallas_call(
        kern,
        grid=(tiles.n_tiles,),
        in_specs=tiles.specs(d) + [
            pl.BlockSpec((tm, att_p.shape[1]), prow),
            pl.BlockSpec((tm, att_s.shape[1]), srow),
            pl.BlockSpec((tm, rnn_p.shape[1]), prow),
            pl.BlockSpec((tm, rnn_s.shape[1]), srow),
            pl.BlockSpec(w_out_b.shape, const),
            pl.BlockSpec(norm2.shape, const),
            pl.BlockSpec(w_router.shape, lambda i: (0, 0, 0)),
            pl.BlockSpec(b_router.shape, const),
        ],
        out_specs=[
            pl.BlockSpec((tm, d), row),
            pl.BlockSpec((tm, d), row),
            pl.BlockSpec((N_EXPERTS, tm), lambda i: (0, i)),
        ],
        out_shape=[
            jax.ShapeDtypeStruct((nt, d), F32),
            jax.ShapeDtypeStruct((nt, d), BF16),
            jax.ShapeDtypeStruct((N_EXPERTS, nt), F32),
        ],
        compiler_params=_params(("parallel",)),
        name="outproj_router",
    )(head, xp, xs, att_p, att_s, rnn_p, rnn_s, w_out_b, norm2, w_router, b_router)


MOE_EXPERTS_PER_STEP = 4


def _moe_body(hn_ref, h1_ref, gates_ref, wg_ref, wu_ref, wd_ref, nf_ref, yp_ref, ys_ref, acc_sc, *, n_prompt_tiles):
    step = pl.program_id(1)

    @pl.when(step == 0)
    def _():
        acc_sc[...] = jnp.zeros(acc_sc.shape, F32)

    x = hn_ref[...]
    gates = gates_ref[...]
    lane = lax.broadcasted_iota(jnp.int32, gates.shape, 1)
    total = None
    for j in range(MOE_EXPERTS_PER_STEP):
        hg = jnp.dot(x, wg_ref[j], preferred_element_type=F32)
        hu = jnp.dot(x, wu_ref[j], preferred_element_type=F32)
        he = (hg * _sigmoid(hg) * hu).astype(BF16)
        out = jnp.dot(he, wd_ref[j], preferred_element_type=F32)
        e = step * MOE_EXPERTS_PER_STEP + j
        gcol = jnp.sum(jnp.where(lane == e, gates, 0.0), axis=1, keepdims=True)
        total = gcol * out if total is None else total + gcol * out
    acc_sc[...] = acc_sc[...] + total

    def final():
        h2 = h1_ref[...] + acc_sc[...]
        ms = jnp.mean(h2 * h2, axis=-1, keepdims=True)
        return h2 * lax.rsqrt(ms + NORM_EPS) * nf_ref[...]

    last = step == pl.num_programs(1) - 1
    is_prompt = pl.program_id(0) < n_prompt_tiles

    @pl.when(last & is_prompt)
    def _():
        yp_ref[...] = final()

    @pl.when(last & jnp.logical_not(is_prompt))
    def _():
        ys_ref[...] = final()


def _moe(hn2, h1, gates, wg_b, wu_b, wd_b, norm_f, np_rows):
    nt, d = h1.shape
    de = wg_b.shape[2]
    tm = next(t for t in (512, ROW_TILE) if np_rows % t == 0 and (nt - np_rows) % t == 0)
    npt, nst = np_rows // tm, (nt - np_rows) // tm
    eps = MOE_EXPERTS_PER_STEP
    row = lambda i, e: (i, 0)
    return pl.pallas_call(
        functools.partial(_moe_body, n_prompt_tiles=npt),
        grid=(nt // tm, N_EXPERTS // eps),
        in_specs=[
            pl.BlockSpec((tm, d), row),
            pl.BlockSpec((tm, d), row),
            pl.BlockSpec((tm, N_EXPERTS), row),
            pl.BlockSpec((eps, d, de), lambda i, e: (e, 0, 0)),
            pl.BlockSpec((eps, d, de), lambda i, e: (e, 0, 0)),
            pl.BlockSpec((eps, de, d), lambda i, e: (e, 0, 0)),
            pl.BlockSpec((1, d), lambda i, e: (0, 0)),
        ],
        out_specs=[
            pl.BlockSpec((tm, d), lambda i, e: (jnp.minimum(i, npt - 1), 0)),
            pl.BlockSpec((tm, d), lambda i, e: (jnp.clip(i - npt, 0, nst - 1), 0)),
        ],
        out_shape=[
            jax.ShapeDtypeStruct((np_rows, d), F32),
            jax.ShapeDtypeStruct((nt - np_rows, d), F32),
        ],
        scratch_shapes=[pltpu.VMEM((tm, d), F32)],
        compiler_params=_params(("arbitrary", "arbitrary")),
        name="moe_dense",
    )(hn2, h1, gates, wg_b, wu_b, wd_b, norm_f)


def _rope_tables(pos):
    half = ROPE_DIM // 2
    inv = jnp.power(ROPE_THETA, -2.0 * jnp.arange(half, dtype=F32) / ROPE_DIM)
    ang = pos[:, None] * inv[None, :]
    cos, sin = jnp.cos(ang), jnp.sin(ang)
    n = pos.shape[0]
    pad = jnp.zeros((n, HEAD_QK - ROPE_DIM), F32)
    ct = jnp.concatenate([cos, cos, pad + 1.0], axis=1)
    s1 = jnp.concatenate([-sin, jnp.zeros_like(sin), pad], axis=1)
    s2 = jnp.concatenate([jnp.zeros_like(sin), sin, pad], axis=1)
    rep = LANES // HEAD_QK
    return jnp.stack([jnp.tile(t, (1, rep)) for t in (ct, s1, s2)])


def _block_diag_halves(w):
    nb, bs, _ = w.shape
    hb = nb // 2
    eye = jnp.eye(hb, dtype=w.dtype)
    halves = [jnp.einsum('nkj,nm->nkmj', w[i * hb:(i + 1) * hb], eye).reshape(hb * bs, hb * bs) for i in range(2)]
    return jnp.stack(halves).astype(BF16)


def kernel(x_prompt, x_sample, cache_k, cache_v, state_conv, state_h, page_table, meta_tokens, norm1, w_in,
           lambda_q1, lambda_k1, lambda_q2, lambda_k2, subln, conv_w, conv_b, rg_w_a, rg_b_a, rg_w_i, rg_b_i,
           lru_lambda, w_out, norm2, w_grp, b_grp, w_rt, b_rt, w_gate, w_up, w_down, norm_f):
    assert w_in.shape[0] == 1, "single-layer trunk only"
    n_batch, seq, d = x_prompt.shape
    n_seq, n_t, _ = x_sample.shape
    assert seq % ROW_TILE == 0 and n_t >= CONV_W - 1
    front = ROW_TILE - N_META
    lp = ROW_TILE + seq
    n_pages, page = page_table.shape[1], cache_k.shape[2]
    past_len = n_pages * page
    ns = n_seq * n_t
    ns_pad = -(-ns // ROW_TILE) * ROW_TILE
    np_rows = n_batch * lp
    rw = lru_lambda.shape[1]

    assert ROW_TILE % n_t == 0
    tiles = _TokenTiles(n_batch, lp // ROW_TILE, ns_pad // ROW_TILE)
    head = jnp.concatenate([jnp.zeros((front, d), F32), meta_tokens.astype(F32)], axis=0)
    xp = x_prompt.reshape(n_batch * seq, d)
    xs = jnp.pad(x_sample.reshape(ns, d), ((0, ns_pad - ns), (0, 0)))
    pos_p = jnp.maximum(jnp.arange(lp) - front, 0)
    pos_s = past_len + jnp.arange(ROW_TILE) % n_t
    cs = _rope_tables(jnp.concatenate([pos_p, pos_s]).astype(F32))

    lamv = jnp.stack([lambda_q1[0], lambda_k1[0], lambda_q2[0], lambda_k2[0]]).astype(F32)
    gain = subln.astype(F32)

    q_all, k_p, k_s, kb_all, v_p, v_s, vb_all, xr_all, xg_all = _inproj(tiles, head, xp, xs, norm1,
                                                                  w_in[0].astype(BF16), cs)

    att_p = _prompt_attention(q_all, kb_all, vb_all, lamv, gain, n_batch, lp, front)
    wa_bd, wi_bd = _block_diag_halves(rg_w_a[0]), _block_diag_halves(rg_w_i[0])
    rnn_w = (conv_w[0], conv_b, wa_bd, rg_b_a, wi_bd, rg_b_i, lru_lambda)
    rnn_p, hl_p = _rglru_prompt(xr_all, xg_all, *rnn_w, n_batch, lp, front)

    sl = slice(np_rows, np_rows + ns)
    q_s = q_all[sl].reshape(n_seq, n_t, ATT_HEADS * 2, HEAD_QK)
    eye = jnp.eye(ATT_HEADS * 2, dtype=BF16)
    qbd = jnp.einsum('bthd,hg->bhtgd', q_s, eye).reshape(n_seq, ATT_HEADS * 2 * n_t, Q_WIDTH)
    att_s = _sample_attention(qbd, k_s[:ns].reshape(n_seq, n_t, Q_WIDTH), v_s[:ns].reshape(n_seq, n_t, ATT_WIDTH),
                              jnp.transpose(cache_k[0], (0, 2, 3, 4, 1)).reshape(-1, Q_WIDTH, page),
                              cache_v[0].reshape(-1, page * ATT_HEADS, HEAD_V),
                              page_table, lamv, gain)
    att_s = jnp.concatenate([att_s.reshape(ns, ATT_WIDTH).astype(BF16),
                             jnp.zeros((ns_pad - ns, ATT_WIDTH), BF16)], axis=0)
    to_tmajor = lambda a: jnp.swapaxes(a[sl].reshape(n_seq, n_t, rw), 0, 1)
    rnn_s, hl_s = _rglru_sample(to_tmajor(xr_all), to_tmajor(xg_all), jnp.swapaxes(state_conv[0], 0, 1),
                                state_h[0].astype(F32), *rnn_w)
    rnn_s = jnp.concatenate([jnp.swapaxes(rnn_s, 0, 1).reshape(ns, rw), jnp.zeros((ns_pad - ns, rw), BF16)], axis=0)

    w_r32 = jnp.concatenate([w_grp[0], w_rt[0], jnp.zeros((d, LANES - N_GROUPS - N_EXPERTS), F32)],
                            axis=1).astype(F32)
    w_r_hi = w_r32.astype(BF16)
    w_router = jnp.stack([w_r_hi, (w_r32 - w_r_hi.astype(F32)).astype(BF16)])
    b_router = jnp.concatenate([b_grp[0], b_rt[0], jnp.zeros((32 - N_GROUPS - N_EXPERTS,), F32)])[:, None]
    h1, hn2, gates_t = _outproj(tiles, head, xp, xs, att_p, att_s, rnn_p, rnn_s, w_out[0].astype(BF16), norm2,
                                w_router, b_router.astype(F32))
    y_p, y_s = _moe(hn2, h1, gates_t.T, w_gate[0].astype(BF16), w_up[0].astype(BF16), w_down[0].astype(BF16),
                    norm_f[None, :], np_rows)

    def prompt_rows(a, skip):
        return a.reshape(n_batch, lp, a.shape[1])[:, skip:]

    y_prompt = prompt_rows(y_p, ROW_TILE)
    y_sample = y_s[:ns].reshape(n_seq, n_t, d)
    k_prompt = prompt_rows(k_p, front).reshape(1, n_batch, N_META + seq, ATT_HEADS, 2, HEAD_QK)
    v_prompt = prompt_rows(v_p, front).reshape(1, n_batch, N_META + seq, ATT_HEADS, HEAD_V)
    k_sample = k_s[:ns].reshape(1, n_seq, n_t, ATT_HEADS, 2, HEAD_QK)
    v_sample = v_s[:ns].reshape(1, n_seq, n_t, ATT_HEADS, HEAD_V)
    conv_prompt = jnp.stack([xr_all[(b + 1) * lp - (CONV_W - 1):(b + 1) * lp] for b in range(n_batch)])[None]
    conv_sample = xr_all[sl].reshape(n_seq, n_t, rw)[:, n_t - (CONV_W - 1):][None]
    h_prompt = hl_p.reshape(n_batch, 8, rw)[:, 7][None]
    h_sample = hl_s[None]
    return (y_prompt, y_sample, k_prompt, v_prompt, k_sample, v_sample, conv_prompt, conv_sample,
            h_prompt, h_sample)
```

```python
import functools
import math

import jax
import jax.numpy as jnp
from jax import lax
from jax.experimental import pallas as pl
from jax.experimental.pallas import tpu as pltpu

F32 = jnp.float32
BF16 = jnp.bfloat16

N_META = 16
ATT_HEADS = 4
HEAD_QK = 64
HEAD_V = 2 * HEAD_QK
ATT_WIDTH = ATT_HEADS * HEAD_V
Q_WIDTH = ATT_HEADS * 2 * HEAD_QK
ROPE_DIM = HEAD_QK // 4
ROPE_THETA = 500000.0
RNN_BLOCKS = 8
CONV_W = 4
LRU_C = 8.0
N_GROUPS = 4
EXP_PER_GROUP = 4
N_EXPERTS = N_GROUPS * EXP_PER_GROUP
NORM_EPS = 1e-6
NEG_INF = -1e30
LAM_INIT = 0.8 - 0.6 * math.exp(-0.3 * 0)

LANES = 128
ROW_TILE = 256
VMEM_LIMIT = 56 * 1024 * 1024
SUBLANES = 8
PAGES_PER_STEP = 32


def _params(sem, vmem=VMEM_LIMIT):
    return pltpu.CompilerParams(dimension_semantics=sem, vmem_limit_bytes=vmem)


def _sigmoid(x):
    return 1.0 / (1.0 + jnp.exp(-x))


def _softplus(x):
    return jnp.maximum(x, 0.0) + jnp.log1p(jnp.exp(-jnp.abs(x)))


def _gelu_tanh(x):
    return 0.5 * x * (1.0 + jnp.tanh(math.sqrt(2.0 / math.pi) * (x + 0.044715 * (x * x * x))))


def _lam_scalar(lamv):
    s1 = jnp.sum(lamv[0:1] * lamv[1:2], axis=-1, keepdims=True)
    s2 = jnp.sum(lamv[2:3] * lamv[3:4], axis=-1, keepdims=True)
    return jnp.exp(s1) - jnp.exp(s2) + LAM_INIT


class _TokenTiles:
    def __init__(self, n_batch, tiles_per_batch, n_sample_tiles):
        self.tpb = tiles_per_batch
        self.n_prompt = n_batch * tiles_per_batch
        self.n_sample = n_sample_tiles
        self.n_tiles = self.n_prompt + n_sample_tiles

    def prompt_block(self, i):
        blk = (i // self.tpb) * (self.tpb - 1) + jnp.maximum(i % self.tpb - 1, 0)
        return jnp.minimum(blk, self.n_prompt - self.n_prompt // self.tpb - 1)

    def sample_block(self, i):
        return jnp.clip(i - self.n_prompt, 0, self.n_sample - 1)

    def specs(self, d):
        tm = ROW_TILE
        return [pl.BlockSpec((tm, d), lambda i: (0, 0)),
                pl.BlockSpec((tm, d), lambda i: (self.prompt_block(i), 0)),
                pl.BlockSpec((tm, d), lambda i: (self.sample_block(i), 0))]

    def select(self, head_ref, xp_ref, xs_ref):
        i = pl.program_id(0)
        x = jnp.where(i % self.tpb == 0, head_ref[...], xp_ref[...])
        return jnp.where(i >= self.n_prompt, xs_ref[...], x)


def _inproj_body(head_ref, xp_ref, xs_ref, g_ref, w_ref, cs_ref, q_ref, kp_ref, ks_ref, kb_ref, vp_ref, vs_ref, vb_ref,
                 xr_ref, xg_ref, *, tiles):
    x = tiles.select(head_ref, xp_ref, xs_ref)
    is_prompt = pl.program_id(0) < tiles.n_prompt
    ms = jnp.mean(x * x, axis=-1, keepdims=True)
    hn = (x * lax.rsqrt(ms + NORM_EPS) * g_ref[...]).astype(BF16)
    proj = jnp.dot(hn, w_ref[...], preferred_element_type=F32)
    ct, s1, s2 = cs_ref[0], cs_ref[1], cs_ref[2]

    def rope(xc):
        return xc * ct + pltpu.roll(xc, LANES - ROPE_DIM // 2, 1) * s1 + pltpu.roll(xc, ROPE_DIM // 2, 1) * s2

    k_chunks = []
    for c in range(Q_WIDTH // LANES):
        sl = slice(c * LANES, (c + 1) * LANES)
        qc = rope(proj[:, c * LANES:(c + 1) * LANES])
        q_ref[:, sl] = (qc * (HEAD_QK ** -0.5)).astype(BF16)
        kc = rope(proj[:, Q_WIDTH + c * LANES:Q_WIDTH + (c + 1) * LANES])
        k_chunks.append(kc)
        kb_ref[:, sl] = kc.astype(BF16)
    k = jnp.concatenate(k_chunks, axis=1)
    v = proj[:, 2 * Q_WIDTH:2 * Q_WIDTH + ATT_WIDTH]
    vb_ref[...] = v.astype(BF16)

    @pl.when(is_prompt)
    def _():
        kp_ref[...] = k
        vp_ref[...] = v

    @pl.when(jnp.logical_not(is_prompt))
    def _():
        ks_ref[...] = k
        vs_ref[...] = v

    r0 = 2 * Q_WIDTH + ATT_WIDTH
    rw = xr_ref.shape[1]
    xr_ref[...] = proj[:, r0:r0 + rw]
    xg_ref[...] = proj[:, r0 + rw:r0 + 2 * rw]


def _inproj(tiles, head, xp, xs, norm1, w_in_b, cs):
    d = head.shape[1]
    tm = ROW_TILE
    nt = tiles.n_tiles * tm
    rw = (w_in_b.shape[1] - 2 * Q_WIDTH - ATT_WIDTH) // 2
    row = lambda i: (i, 0)
    const = lambda i: (0, 0)
    cs_tile = lambda i: (0, jnp.where(i < tiles.n_prompt, i % tiles.tpb, tiles.tpb), 0)
    npr, nsr = tiles.n_prompt * tm, tiles.n_sample * tm
    prow = lambda i: (jnp.minimum(i, tiles.n_prompt - 1), 0)
    srow = lambda i: (tiles.sample_block(i), 0)
    outs = [
        ((nt, Q_WIDTH), BF16, row),
        ((npr, Q_WIDTH), F32, prow),
        ((nsr, Q_WIDTH), F32, srow),
        ((nt, Q_WIDTH), BF16, row),
        ((npr, ATT_WIDTH), F32, prow),
        ((nsr, ATT_WIDTH), F32, srow),
        ((nt, ATT_WIDTH), BF16, row),
        ((nt, rw), F32, row),
        ((nt, rw), F32, row),
    ]
    return pl.pallas_call(
        functools.partial(_inproj_body, tiles=tiles),
        grid=(tiles.n_tiles,),
        in_specs=tiles.specs(d) + [
            pl.BlockSpec((1, d), const),
            pl.BlockSpec(w_in_b.shape, const),
            pl.BlockSpec((3, tm, LANES), cs_tile),
        ],
        out_specs=[pl.BlockSpec((tm, s[1]), m) for s, _, m in outs],
        out_shape=[jax.ShapeDtypeStruct(s, t) for s, t, _ in outs],
        compiler_params=_params(("arbitrary",)),
        name="inproj",
    )(head, xp, xs, norm1, w_in_b, cs)


def _flash_body(q_ref, k_ref, v_ref, lamv_ref, gain_ref, o_ref, q2_sc, m_sc, acc_sc, *, tq, front):
    qi = pl.program_id(1)
    lane = lax.broadcasted_iota(jnp.int32, (tq, LANES), 1)
    for h in range(ATT_HEADS):
        q = q_ref[:, h * LANES:(h + 1) * LANES]
        zero = jnp.zeros_like(q)
        q2_sc[h] = jnp.concatenate([jnp.where(lane < HEAD_QK, q, zero), jnp.where(lane >= HEAD_QK, q, zero)], axis=0)
    m_sc[...] = jnp.full(m_sc.shape, NEG_INF, F32)
    acc_sc[...] = jnp.zeros(acc_sc.shape, F32)

    def step(ki, width, masked):
        start = pl.multiple_of(ki * tq, tq)
        n_ch = width // LANES
        ones = jnp.ones((width, LANES), BF16)
        for h in range(ATT_HEADS):
            k = k_ref[pl.ds(start, width), h * LANES:(h + 1) * LANES]
            v = v_ref[pl.ds(start, width), h * LANES:(h + 1) * LANES]
            s = lax.dot_general(q2_sc[h], k, (((1,), (1,)), ((), ())), preferred_element_type=F32)
            if masked:
                row = lax.broadcasted_iota(jnp.int32, s.shape, 0)
                col = lax.broadcasted_iota(jnp.int32, s.shape, 1)
                qpos = qi * tq + jnp.where(row >= tq, row - tq, row)
                kpos = ki * tq + col
                s = jnp.where(kpos <= qpos, jnp.where(kpos >= front, s, NEG_INF), NEG_INF)
            chunks = [s[:, c * LANES:(c + 1) * LANES] for c in range(n_ch)]
            m_prev = m_sc[h]
            m_cur = jnp.max(functools.reduce(jnp.maximum, chunks), axis=1, keepdims=True)
            m_next = jnp.maximum(m_prev, m_cur)
            p = jnp.concatenate([jnp.exp(c - m_next) for c in chunks], axis=1).astype(BF16)
            alpha = jnp.exp(m_prev - m_next)
            pv = jnp.dot(p, jnp.concatenate([v, ones], axis=1), preferred_element_type=F32)
            acc_sc[h] = jnp.concatenate([alpha, alpha], axis=1) * acc_sc[h] + pv
            m_sc[h] = m_next

    step(0, tq, True)
    n_plain = jnp.maximum(qi - 1, 0)

    def body(j, carry):
        step(1 + 2 * j, 2 * tq, False)
        return carry

    lax.fori_loop(0, n_plain // 2, body, 0)

    @pl.when(n_plain % 2 == 1)
    def _():
        step(qi - 1, tq, False)

    @pl.when(qi > 0)
    def _():
        step(qi, tq, True)

    lam = _lam_scalar(lamv_ref[...])
    for h in range(ATT_HEADS):
        acc = acc_sc[h]
        o = acc[:, :HEAD_V] / acc[:, HEAD_V:]
        d = o[:tq] - lam * o[tq:]
        ms = jnp.mean(d * d, axis=-1, keepdims=True)
        o_ref[:, h * HEAD_V:(h + 1) * HEAD_V] = (
            d * lax.rsqrt(ms + NORM_EPS) * gain_ref[...] * (1.0 - LAM_INIT)).astype(BF16)


def _prompt_attention(q_all, kb_all, vb_all, lamv, gain, n_batch, lp, front):
    tq = ROW_TILE
    nq = lp // tq
    kern = functools.partial(_flash_body, tq=tq, front=front)
    return pl.pallas_call(
        kern,
        grid=(n_batch, nq),
        in_specs=[
            pl.BlockSpec((tq, Q_WIDTH), lambda b, i: (b * nq + i, 0)),
            pl.BlockSpec((lp, Q_WIDTH), lambda b, i: (b, 0)),
            pl.BlockSpec((lp, ATT_WIDTH), lambda b, i: (b, 0)),
            pl.BlockSpec(lamv.shape, lambda b, i: (0, 0)),
            pl.BlockSpec(gain.shape, lambda b, i: (0, 0)),
        ],
        out_specs=pl.BlockSpec((tq, ATT_WIDTH), lambda b, i: (b * nq + i, 0)),
        out_shape=jax.ShapeDtypeStruct((n_batch * lp, ATT_WIDTH), BF16),
        scratch_shapes=[
            pltpu.VMEM((ATT_HEADS, 2 * tq, LANES), BF16),
            pltpu.VMEM((ATT_HEADS, 2 * tq, LANES), F32),
            pltpu.VMEM((ATT_HEADS, 2 * tq, 2 * HEAD_V), F32),
        ],
        compiler_params=_params(("parallel", "arbitrary")),
        name="prompt_attn",
    )(q_all, kb_all, vb_all, lamv, gain)


def _paged_body(pt_ref, qbd_ref, kn_ref, vn_ref, lamv_ref, gain_ref, *rest, n_pg, n_chunks, n_t):
    del pt_ref
    k_refs = rest[:n_pg]
    v_refs = rest[n_pg:2 * n_pg]
    o_ref = rest[2 * n_pg]
    m_sc, l_sc, acc_sc = rest[2 * n_pg + 1:]
    j = pl.program_id(1)
    n_rows = qbd_ref.shape[0]

    @pl.when(j == 0)
    def _():
        m_sc[...] = jnp.full(m_sc.shape, NEG_INF, F32)
        l_sc[...] = jnp.zeros(l_sc.shape, F32)
        acc_sc[...] = jnp.zeros(acc_sc.shape, F32)

    qbd = qbd_ref[...]
    chunks = [jnp.dot(qbd, k_refs[i][...].astype(BF16), preferred_element_type=F32) for i in range(n_pg)]
    m_prev = m_sc[...]
    m_cur = jnp.max(functools.reduce(jnp.maximum, chunks), axis=1, keepdims=True)
    m_next = jnp.maximum(m_prev, m_cur)
    p = [jnp.exp(c - m_next) for c in chunks]
    alpha = jnp.exp(m_prev - m_next)
    l_sc[...] = alpha * l_sc[...] + jnp.sum(functools.reduce(jnp.add, p), axis=1, keepdims=True)
    page = v_refs[0].shape[0] // ATT_HEADS
    hr = 2 * n_t
    pv = []
    for h in range(ATT_HEADS):
        d = None
        for i in range(n_pg):
            vh = v_refs[i][pl.ds(h, page, stride=ATT_HEADS), :].astype(BF16)
            t = jnp.dot(p[i][h * hr:(h + 1) * hr].astype(BF16), vh, preferred_element_type=F32)
            d = t if d is None else d + t
        pv.append(d)
    acc_sc[...] = acc_sc[...] * alpha + jnp.concatenate(pv, axis=0)
    m_sc[...] = m_next

    @pl.when(j == n_chunks - 1)
    def _():
        qf = qbd.astype(F32)
        kn = kn_ref[...].astype(BF16).astype(F32)
        vn = vn_ref[...].astype(BF16).astype(F32)
        row_t = lax.broadcasted_iota(jnp.int32, (n_rows, 1), 0) % n_t
        s_new = []
        for t2 in range(n_t):
            st = jnp.sum(qf * kn[t2:t2 + 1, :], axis=1, keepdims=True)
            s_new.append(jnp.where(row_t >= t2, st, NEG_INF))
        m_old = m_sc[...]
        m_fin = m_old
        for st in s_new:
            m_fin = jnp.maximum(m_fin, st)
        a_fin = jnp.exp(m_old - m_fin)
        l_fin = a_fin * l_sc[...]
        acc = acc_sc[...] * a_fin
        for t2 in range(n_t):
            pt = jnp.exp(s_new[t2] - m_fin)
            l_fin = l_fin + pt
            v_rows = jnp.concatenate(
                [jnp.broadcast_to(vn[t2:t2 + 1, h * HEAD_V:(h + 1) * HEAD_V], (2 * n_t, HEAD_V))
                 for h in range(ATT_HEADS)], axis=0)
            acc = acc + pt * v_rows
        o = acc / l_fin
        lam = _lam_scalar(lamv_ref[...])
        for h in range(ATT_HEADS):
            r0 = h * 2 * n_t
            d = o[r0:r0 + n_t] - lam * o[r0 + n_t:r0 + 2 * n_t]
            ms = jnp.mean(d * d, axis=-1, keepdims=True)
            o_ref[:, h * HEAD_V:(h + 1) * HEAD_V] = d * lax.rsqrt(ms + NORM_EPS) * gain_ref[...] * (1.0 - LAM_INIT)


def _sample_attention(qbd, k_new, v_new, cache_kt, cache_vm, page_table, lamv, gain):
    n_seq, n_rows, kw = qbd.shape
    n_t = k_new.shape[1]
    n_pages = page_table.shape[1]
    n_pg = math.gcd(n_pages, PAGES_PER_STEP)
    n_chunks = n_pages // n_pg
    pt_flat = page_table.reshape(-1)

    def page_map(i):
        return lambda b, j, pt: (pt[b * n_pages + j * n_pg + i], 0, 0)

    seq3 = lambda b, j, pt: (b, 0, 0)
    const2 = lambda b, j, pt: (0, 0)
    kern = functools.partial(_paged_body, n_pg=n_pg, n_chunks=n_chunks, n_t=n_t)
    grid_spec = pltpu.PrefetchScalarGridSpec(
        num_scalar_prefetch=1,
        grid=(n_seq, n_chunks),
        in_specs=[
            pl.BlockSpec((None, n_rows, kw), seq3),
            pl.BlockSpec((None, n_t, kw), seq3),
            pl.BlockSpec((None, n_t, kw), seq3),
            pl.BlockSpec(lamv.shape, const2),
            pl.BlockSpec(gain.shape, const2),
        ] + [pl.BlockSpec((None,) + cache_kt.shape[1:], page_map(i)) for i in range(n_pg)]
          + [pl.BlockSpec((None,) + cache_vm.shape[1:], page_map(i)) for i in range(n_pg)],
        out_specs=pl.BlockSpec((None, n_t, ATT_WIDTH), seq3),
        scratch_shapes=[
            pltpu.VMEM((n_rows, LANES), F32),
            pltpu.VMEM((n_rows, LANES), F32),
            pltpu.VMEM((n_rows, HEAD_V), F32),
        ],
    )
    return pl.pallas_call(
        kern,
        grid_spec=grid_spec,
        out_shape=jax.ShapeDtypeStruct((n_seq, n_t, ATT_WIDTH), F32),
        compiler_params=_params(("parallel", "arbitrary")),
        name="sample_attn",
    )(pt_flat, qbd, k_new, v_new, lamv, gain, *([cache_kt] * n_pg), *([cache_vm] * n_pg))


def _lru_gates(xc, wa_ref, ba, wi_ref, bi, lam_row):
    half = xc.shape[1] // 2
    xb = xc.astype(BF16)

    def blockdiag(w_ref):
        return jnp.concatenate([
            jnp.dot(xb[:, :half], w_ref[0], preferred_element_type=F32),
            jnp.dot(xb[:, half:], w_ref[1], preferred_element_type=F32)], axis=1)

    gate_r = _sigmoid(blockdiag(wa_ref) + ba)
    gate_i = _sigmoid(blockdiag(wi_ref) + bi)
    log_a = -LRU_C * gate_r * _softplus(-lam_row)
    a = jnp.exp(log_a)
    th = jnp.tanh(log_a)
    om = -2.0 * th / (1.0 - th)
    root = om * lax.rsqrt(jnp.maximum(om, jnp.finfo(jnp.float32).tiny))
    u = root * (gate_i * xc)
    return a, u


def _rglru_prompt_body(xr_ref, xg_ref, cw_ref, cb_ref, wa_ref, ba_ref, wi_ref, bi_ref, lam_ref,
                       rnn_ref, hl_ref, prev_sc, h_sc, *, tc, front):
    c = pl.program_id(1)

    @pl.when(c == 0)
    def _():
        prev_sc[...] = jnp.zeros(prev_sc.shape, F32)
        h_sc[...] = jnp.zeros(h_sc.shape, F32)

    x = xr_ref[...]
    prev = prev_sc[...]
    rw = x.shape[1]
    sub = SUBLANES
    top_row = lax.broadcasted_iota(jnp.int32, (sub, 1), 0)

    def delayed(d):
        r = pltpu.roll(x, d, 0)
        top = jnp.where(top_row < d, pltpu.roll(prev, d, 0), r[0:sub])
        return jnp.concatenate([top, r[sub:]], axis=0)

    cw = cw_ref[...]
    y = cb_ref[...] + delayed(3) * cw[0:1]
    y = y + delayed(2) * cw[1:2]
    y = y + delayed(1) * cw[2:3]
    y = y + x * cw[3:4]
    prev_sc[...] = x[tc - sub:tc]
    a, u = _lru_gates(y, wa_ref, ba_ref[...], wi_ref, bi_ref[...], lam_ref[...])
    row = lax.broadcasted_iota(jnp.int32, (tc, 1), 0)
    u = jnp.where(c * tc + row >= front, u, 0.0)
    a = a.reshape(tc // sub, sub, rw)
    u = u.reshape(tc // sub, sub, rw)
    in_group = lax.broadcasted_iota(jnp.int32, (1, sub, 1), 1)
    d = 1
    while d < sub:
        keep = in_group >= d
        u = jnp.where(keep, a * pltpu.roll(u, d, 1) + u, u)
        a = jnp.where(keep, a * pltpu.roll(a, d, 1), a)
        d *= 2
    carry = h_sc[0:1, :]
    groups = []
    for g in range(tc // sub):
        hg = a[g] * carry + u[g]
        groups.append(hg)
        carry = hg[sub - 1:sub]
    h = jnp.concatenate(groups, axis=0)
    h_sc[0:1, :] = carry
    hl_ref[...] = groups[-1]
    rnn_ref[...] = (_gelu_tanh(xg_ref[...]) * h).astype(BF16)


def _rglru_prompt(xr_all, xg_all, cw, cb, wa_bd, ba, wi_bd, bi, lam, n_batch, lp, front):
    tc = ROW_TILE
    nc = lp // tc
    rw = xr_all.shape[1]
    kern = functools.partial(_rglru_prompt_body, tc=tc, front=front)
    row = lambda b, c: (b * nc + c, 0)
    const2 = lambda b, c: (0, 0)
    const3 = lambda b, c: (0, 0, 0)
    return pl.pallas_call(
        kern,
        grid=(n_batch, nc),
        in_specs=[
            pl.BlockSpec((tc, rw), row),
            pl.BlockSpec((tc, rw), row),
            pl.BlockSpec(cw.shape, const2),
            pl.BlockSpec(cb.shape, const2),
            pl.BlockSpec(wa_bd.shape, const3),
            pl.BlockSpec(ba.shape, const2),
            pl.BlockSpec(wi_bd.shape, const3),
            pl.BlockSpec(bi.shape, const2),
            pl.BlockSpec(lam.shape, const2),
        ],
        out_specs=[
            pl.BlockSpec((tc, rw), row),
            pl.BlockSpec((SUBLANES, rw), lambda b, c: (b, 0)),
        ],
        out_shape=[
            jax.ShapeDtypeStruct((n_batch * lp, rw), BF16),
            jax.ShapeDtypeStruct((n_batch * SUBLANES, rw), F32),
        ],
        scratch_shapes=[pltpu.VMEM((SUBLANES, rw), F32), pltpu.VMEM((SUBLANES, rw), F32)],
        compiler_params=_params(("parallel", "arbitrary")),
        name="rglru_prompt",
    )(xr_all, xg_all, cw, cb, wa_bd, ba, wi_bd, bi, lam)


def _rglru_sample_body(xr_ref, xg_ref, cs_ref, h0_ref, cw_ref, cb_ref, wa_ref, ba_ref, wi_ref, bi_ref, lam_ref,
                       rnn_ref, hl_ref):
    n_t = xr_ref.shape[0]
    n_c = cs_ref.shape[0]
    rows = [cs_ref[i] for i in range(n_c)] + [xr_ref[t] for t in range(n_t)]
    cw = cw_ref[...]
    h = h0_ref[...]
    for t in range(n_t):
        y = cb_ref[...] + rows[t] * cw[0:1]
        for j in range(1, CONV_W):
            y = y + rows[t + j] * cw[j:j + 1]
        a, u = _lru_gates(y, wa_ref, ba_ref[...], wi_ref, bi_ref[...], lam_ref[...])
        h = a * h + u
        rnn_ref[t] = (_gelu_tanh(xg_ref[t]) * h).astype(BF16)
    hl_ref[...] = h


def _rglru_sample(xr_t, xg_t, conv_t, h0, cw, cb, wa_bd, ba, wi_bd, bi, lam):
    n_t, n_seq, rw = xr_t.shape
    return pl.pallas_call(
        _rglru_sample_body,
        out_shape=[
            jax.ShapeDtypeStruct((n_t, n_seq, rw), BF16),
            jax.ShapeDtypeStruct((n_seq, rw), F32),
        ],
        compiler_params=pltpu.CompilerParams(vmem_limit_bytes=VMEM_LIMIT),
        name="rglru_sample",
    )(xr_t, xg_t, conv_t, h0, cw, cb, wa_bd, ba, wi_bd, bi, lam)


def _route(logits):
    g = [logits[i:i + 1] for i in range(N_GROUPS)]
    gmax = functools.reduce(jnp.maximum, g)
    g_idx = jnp.full(gmax.shape, N_GROUPS - 1, jnp.int32)
    for i in range(N_GROUPS - 2, -1, -1):
        g_idx = jnp.where(g[i] == gmax, i, g_idx)
    g_sel = 1.0 / functools.reduce(jnp.add, [jnp.exp(gi - gmax) for gi in g])
    e = []
    for j in range(EXP_PER_GROUP):
        ej = logits[N_GROUPS + j:N_GROUPS + j + 1]
        for grp in range(1, N_GROUPS):
            r = N_GROUPS + grp * EXP_PER_GROUP + j
            ej = jnp.where(g_idx == grp, logits[r:r + 1], ej)
        e.append(ej)
    v1 = functools.reduce(jnp.maximum, e)
    i1 = jnp.full(v1.shape, EXP_PER_GROUP - 1, jnp.int32)
    for j in range(EXP_PER_GROUP - 2, -1, -1):
        i1 = jnp.where(e[j] == v1, j, i1)
    rest = [jnp.where(i1 == j, -jnp.inf, e[j]) for j in range(EXP_PER_GROUP)]
    v2 = functools.reduce(jnp.maximum, rest)
    i2 = jnp.full(v2.shape, EXP_PER_GROUP - 1, jnp.int32)
    for j in range(EXP_PER_GROUP - 2, -1, -1):
        i2 = jnp.where(rest[j] == v2, j, i2)
    z = jnp.exp(v2 - v1)
    w1 = g_sel / (1.0 + z)
    w2 = g_sel * z / (1.0 + z)
    e1 = g_idx * EXP_PER_GROUP + i1
    e2 = g_idx * EXP_PER_GROUP + i2
    rows = [jnp.where(e1 == x, w1, 0.0) + jnp.where(e2 == x, w2, 0.0) for x in range(N_EXPERTS)]
    return jnp.concatenate(rows, axis=0)


def _outproj_body(head_ref, xp_ref, xs_ref, attp_ref, atts_ref, rnnp_ref, rnns_ref, wo_ref, g2_ref, wr_ref, br_ref,
                  h1_ref, hn_ref, gates_ref, *, tiles):
    is_prompt = pl.program_id(0) < tiles.n_prompt
    aw = attp_ref.shape[1]
    att = jnp.where(is_prompt, attp_ref[...], atts_ref[...])
    rnn = jnp.where(is_prompt, rnnp_ref[...], rnns_ref[...])
    mix = jnp.dot(att, wo_ref[0:aw, :], preferred_element_type=F32)
    mix = mix + jnp.dot(rnn, wo_ref[aw:, :], preferred_element_type=F32)
    h1 = tiles.select(head_ref, xp_ref, xs_ref) + mix
    h1_ref[...] = h1
    ms = jnp.mean(h1 * h1, axis=-1, keepdims=True)
    hn = h1 * lax.rsqrt(ms + NORM_EPS) * g2_ref[...]
    hn_hi = hn.astype(BF16)
    hn_ref[...] = hn_hi
    hn_lo = (hn - hn_hi.astype(F32)).astype(BF16)
    lg = jnp.dot(hn_hi, wr_ref[0], preferred_element_type=F32)
    lg = lg + jnp.dot(hn_lo, wr_ref[0], preferred_element_type=F32)
    lg = lg + jnp.dot(hn_hi, wr_ref[1], preferred_element_type=F32)
    logits = jnp.transpose(lg)[0:br_ref.shape[0]] + br_ref[...]
    gates_ref[...] = _route(logits)


def _outproj(tiles, head, xp, xs, att_p, att_s, rnn_p, rnn_s, w_out_b, norm2, w_router, b_router):
    d = head.shape[1]
    tm = ROW_TILE
    nt = tiles.n_tiles * tm
    npt = tiles.n_prompt
    row = lambda i: (i, 0)
    const = lambda i: (0, 0)
    prow = lambda i: (jnp.minimum(i, npt - 1), 0)
    srow = lambda i: (tiles.sample_block(i), 0)
    kern = functools.partial(_outproj_body, tiles=tiles)
    return pl.pallas_call(
        kern,
        grid=(tiles.n_tiles,),
        in_specs=tiles.specs(d) + [
            pl.BlockSpec((tm, att_p.shape[1]), prow),
            pl.BlockSpec((tm, att_s.shape[1]), srow),
            pl.BlockSpec((tm, rnn_p.shape[1]), prow),
            pl.BlockSpec((tm, rnn_s.shape[1]), srow),
            pl.BlockSpec(w_out_b.shape, const),
            pl.BlockSpec(norm2.shape, const),
            pl.BlockSpec(w_router.shape, lambda i: (0, 0, 0)),
            pl.BlockSpec(b_router.shape, const),
        ],
        out_specs=[
            pl.BlockSpec((tm, d), row),
            pl.BlockSpec((tm, d), row),
            pl.BlockSpec((N_EXPERTS, tm), lambda i: (0, i)),
        ],
        out_shape=[
            jax.ShapeDtypeStruct((nt, d), F32),
            jax.ShapeDtypeStruct((nt, d), BF16),
            jax.ShapeDtypeStruct((N_EXPERTS, nt), F32),
        ],
        compiler_params=_params(("parallel",)),
        name="outproj_router",
    )(head, xp, xs, att_p, att_s, rnn_p, rnn_s, w_out_b, norm2, w_router, b_router)


MOE_EXPERTS_PER_STEP = 4


def _moe_body(hn_ref, h1_ref, gates_ref, wg_ref, wu_ref, wd_ref, nf_ref, yp_ref, ys_ref, acc_sc, *, n_prompt_tiles):
    step = pl.program_id(1)

    @pl.when(step == 0)
    def _():
        acc_sc[...] = jnp.zeros(acc_sc.shape, F32)

    x = hn_ref[...]
    gates = gates_ref[...]
    lane = lax.broadcasted_iota(jnp.int32, gates.shape, 1)
    total = None
    for j in range(MOE_EXPERTS_PER_STEP):
        hg = jnp.dot(x, wg_ref[j], preferred_element_type=F32)
        hu = jnp.dot(x, wu_ref[j], preferred_element_type=F32)
        he = (hg * _sigmoid(hg) * hu).astype(BF16)
        out = jnp.dot(he, wd_ref[j], preferred_element_type=F32)
        e = step * MOE_EXPERTS_PER_STEP + j
        gcol = jnp.sum(jnp.where(lane == e, gates, 0.0), axis=1, keepdims=True)
        total = gcol * out if total is None else total + gcol * out
    acc_sc[...] = acc_sc[...] + total

    def final():
        h2 = h1_ref[...] + acc_sc[...]
        ms = jnp.mean(h2 * h2, axis=-1, keepdims=True)
        return h2 * lax.rsqrt(ms + NORM_EPS) * nf_ref[...]

    last = step == pl.num_programs(1) - 1
    is_prompt = pl.program_id(0) < n_prompt_tiles

    @pl.when(last & is_prompt)
    def _():
        yp_ref[...] = final()

    @pl.when(last & jnp.logical_not(is_prompt))
    def _():
        ys_ref[...] = final()


def _moe(hn2, h1, gates, wg_b, wu_b, wd_b, norm_f, np_rows):
    nt, d = h1.shape
    de = wg_b.shape[2]
    tm = next(t for t in (512, ROW_TILE) if np_rows % t == 0 and (nt - np_rows) % t == 0)
    npt, nst = np_rows // tm, (nt - np_rows) // tm
    eps = MOE_EXPERTS_PER_STEP
    row = lambda i, e: (i, 0)
    return pl.pallas_call(
        functools.partial(_moe_body, n_prompt_tiles=npt),
        grid=(nt // tm, N_EXPERTS // eps),
        in_specs=[
            pl.BlockSpec((tm, d), row),
            pl.BlockSpec((tm, d), row),
            pl.BlockSpec((tm, N_EXPERTS), row),
            pl.BlockSpec((eps, d, de), lambda i, e: (e, 0, 0)),
            pl.BlockSpec((eps, d, de), lambda i, e: (e, 0, 0)),
            pl.BlockSpec((eps, de, d), lambda i, e: (e, 0, 0)),
            pl.BlockSpec((1, d), lambda i, e: (0, 0)),
        ],
        out_specs=[
            pl.BlockSpec((tm, d), lambda i, e: (jnp.minimum(i, npt - 1), 0)),
            pl.BlockSpec((tm, d), lambda i, e: (jnp.clip(i - npt, 0, nst - 1), 0)),
        ],
        out_shape=[
            jax.ShapeDtypeStruct((np_rows, d), F32),
            jax.ShapeDtypeStruct((nt - np_rows, d), F32),
        ],
        scratch_shapes=[pltpu.VMEM((tm, d), F32)],
        compiler_params=_params(("arbitrary", "arbitrary")),
        name="moe_dense",
    )(hn2, h1, gates, wg_b, wu_b, wd_b, norm_f)


def _rope_tables(pos):
    half = ROPE_DIM // 2
    inv = jnp.power(ROPE_THETA, -2.0 * jnp.arange(half, dtype=F32) / ROPE_DIM)
    ang = pos[:, None] * inv[None, :]
    cos, sin = jnp.cos(ang), jnp.sin(ang)
    n = pos.shape[0]
    pad = jnp.zeros((n, HEAD_QK - ROPE_DIM), F32)
    ct = jnp.concatenate([cos, cos, pad + 1.0], axis=1)
    s1 = jnp.concatenate([-sin, jnp.zeros_like(sin), pad], axis=1)
    s2 = jnp.concatenate([jnp.zeros_like(sin), sin, pad], axis=1)
    rep = LANES // HEAD_QK
    return jnp.stack([jnp.tile(t, (1, rep)) for t in (ct, s1, s2)])


def _block_diag_halves(w):
    nb, bs, _ = w.shape
    hb = nb // 2
    eye = jnp.eye(hb, dtype=w.dtype)
    halves = [jnp.einsum('nkj,nm->nkmj', w[i * hb:(i + 1) * hb], eye).reshape(hb * bs, hb * bs) for i in range(2)]
    return jnp.stack(halves).astype(BF16)


def kernel(x_prompt, x_sample, cache_k, cache_v, state_conv, state_h, page_table, meta_tokens, norm1, w_in,
           lambda_q1, lambda_k1, lambda_q2, lambda_k2, subln, conv_w, conv_b, rg_w_a, rg_b_a, rg_w_i, rg_b_i,
           lru_lambda, w_out, norm2, w_grp, b_grp, w_rt, b_rt, w_gate, w_up, w_down, norm_f):
    assert w_in.shape[0] == 1, "single-layer trunk only"
    n_batch, seq, d = x_prompt.shape
    n_seq, n_t, _ = x_sample.shape
    assert seq % ROW_TILE == 0 and n_t >= CONV_W - 1
    front = ROW_TILE - N_META
    lp = ROW_TILE + seq
    n_pages, page = page_table.shape[1], cache_k.shape[2]
    past_len = n_pages * page
    ns = n_seq * n_t
    ns_pad = -(-ns // ROW_TILE) * ROW_TILE
    np_rows = n_batch * lp
    rw = lru_lambda.shape[1]

    assert ROW_TILE % n_t == 0
    tiles = _TokenTiles(n_batch, lp // ROW_TILE, ns_pad // ROW_TILE)
    head = jnp.concatenate([jnp.zeros((front, d), F32), meta_tokens.astype(F32)], axis=0)
    xp = x_prompt.reshape(n_batch * seq, d)
    xs = jnp.pad(x_sample.reshape(ns, d), ((0, ns_pad - ns), (0, 0)))
    pos_p = jnp.maximum(jnp.arange(lp) - front, 0)
    pos_s = past_len + jnp.arange(ROW_TILE) % n_t
    cs = _rope_tables(jnp.concatenate([pos_p, pos_s]).astype(F32))

    lamv = jnp.stack([lambda_q1[0], lambda_k1[0], lambda_q2[0], lambda_k2[0]]).astype(F32)
    gain = subln.astype(F32)

    q_all, k_p, k_s, kb_all, v_p, v_s, vb_all, xr_all, xg_all = _inproj(tiles, head, xp, xs, norm1,
                                                                  w_in[0].astype(BF16), cs)

    att_p = _prompt_attention(q_all, kb_all, vb_all, lamv, gain, n_batch, lp, front)
    wa_bd, wi_bd = _block_diag_halves(rg_w_a[0]), _block_diag_halves(rg_w_i[0])
    rnn_w = (conv_w[0], conv_b, wa_bd, rg_b_a, wi_bd, rg_b_i, lru_lambda)
    rnn_p, hl_p = _rglru_prompt(xr_all, xg_all, *rnn_w, n_batch, lp, front)

    sl = slice(np_rows, np_rows + ns)
    q_s = q_all[sl].reshape(n_seq, n_t, ATT_HEADS * 2, HEAD_QK)
    eye = jnp.eye(ATT_HEADS * 2, dtype=BF16)
    qbd = jnp.einsum('bthd,hg->bhtgd', q_s, eye).reshape(n_seq, ATT_HEADS * 2 * n_t, Q_WIDTH)
    att_s = _sample_attention(qbd, k_s[:ns].reshape(n_seq, n_t, Q_WIDTH), v_s[:ns].reshape(n_seq, n_t, ATT_WIDTH),
                              jnp.transpose(cache_k[0], (0, 2, 3, 4, 1)).reshape(-1, Q_WIDTH, page),
                              cache_v[0].reshape(-1, page * ATT_HEADS, HEAD_V),
                              page_table, lamv, gain)
    att_s = jnp.concatenate([att_s.reshape(ns, ATT_WIDTH).astype(BF16),
                             jnp.zeros((ns_pad - ns, ATT_WIDTH), BF16)], axis=0)
    to_tmajor = lambda a: jnp.swapaxes(a[sl].reshape(n_seq, n_t, rw), 0, 1)
    rnn_s, hl_s = _rglru_sample(to_tmajor(xr_all), to_tmajor(xg_all), jnp.swapaxes(state_conv[0], 0, 1),
                                state_h[0].astype(F32), *rnn_w)
    rnn_s = jnp.concatenate([jnp.swapaxes(rnn_s, 0, 1).reshape(ns, rw), jnp.zeros((ns_pad - ns, rw), BF16)], axis=0)

    w_r32 = jnp.concatenate([w_grp[0], w_rt[0], jnp.zeros((d, LANES - N_GROUPS - N_EXPERTS), F32)],
                            axis=1).astype(F32)
    w_r_hi = w_r32.astype(BF16)
    w_router = jnp.stack([w_r_hi, (w_r32 - w_r_hi.astype(F32)).astype(BF16)])
    n_logit = N_GROUPS + N_EXPERTS
    n_logit_rows = -(-n_logit // SUBLANES) * SUBLANES
    b_router = jnp.concatenate([b_grp[0], b_rt[0], jnp.zeros((n_logit_rows - n_logit,), F32)])[:, None]
    h1, hn2, gates_t = _outproj(tiles, head, xp, xs, att_p, att_s, rnn_p, rnn_s, w_out[0].astype(BF16), norm2,
                                w_router, b_router.astype(F32))
    y_p, y_s = _moe(hn2, h1, gates_t.T, w_gate[0].astype(BF16), w_up[0].astype(BF16), w_down[0].astype(BF16),
                    norm_f[None, :], np_rows)

    def prompt_rows(a, skip):
        return a.reshape(n_batch, lp, a.shape[1])[:, skip:]

    y_prompt = prompt_rows(y_p, ROW_TILE)
    y_sample = y_s[:ns].reshape(n_seq, n_t, d)
    k_prompt = prompt_rows(k_p, front).reshape(1, n_batch, N_META + seq, ATT_HEADS, 2, HEAD_QK)
    v_prompt = prompt_rows(v_p, front).reshape(1, n_batch, N_META + seq, ATT_HEADS, HEAD_V)
    k_sample = k_s[:ns].reshape(1, n_seq, n_t, ATT_HEADS, 2, HEAD_QK)
    v_sample = v_s[:ns].reshape(1, n_seq, n_t, ATT_HEADS, HEAD_V)
    conv_prompt = jnp.stack([xr_all[(b + 1) * lp - (CONV_W - 1):(b + 1) * lp] for b in range(n_batch)])[None]
    conv_sample = xr_all[sl].reshape(n_seq, n_t, rw)[:, n_t - (CONV_W - 1):][None]
    h_prompt = hl_p.reshape(n_batch, SUBLANES, rw)[:, SUBLANES - 1][None]
    h_sample = hl_s[None]
    return (y_prompt, y_sample, k_prompt, v_prompt, k_sample, v_sample, conv_prompt, conv_sample,
            h_prompt, h_sample)
```

```python
import functools
import math

import jax
import jax.numpy as jnp
from jax import lax
from jax.experimental import pallas as pl
from jax.experimental.pallas import tpu as pltpu

F32 = jnp.float32
BF16 = jnp.bfloat16

N_META = 16
ATT_HEADS = 4
HEAD_QK = 64
HEAD_V = 2 * HEAD_QK
ATT_WIDTH = ATT_HEADS * HEAD_V
Q_WIDTH = ATT_HEADS * 2 * HEAD_QK
ROPE_DIM = HEAD_QK // 4
ROPE_THETA = 500000.0
RNN_BLOCKS = 8
CONV_W = 4
LRU_C = 8.0
N_GROUPS = 4
EXP_PER_GROUP = 4
N_EXPERTS = N_GROUPS * EXP_PER_GROUP
NORM_EPS = 1e-6
NEG_INF = -1e30
LAM_INIT = 0.8 - 0.6 * math.exp(-0.3 * 0)

LANES = 128
ROW_TILE = 256
VMEM_LIMIT = 56 * 1024 * 1024
SUBLANES = 8
PAGES_PER_STEP = 32


def _params(sem, vmem=VMEM_LIMIT):
    return pltpu.CompilerParams(dimension_semantics=sem, vmem_limit_bytes=vmem)


def _sigmoid(x):
    return 1.0 / (1.0 + jnp.exp(-x))


def _softplus(x):
    return jnp.maximum(x, 0.0) + jnp.log1p(jnp.exp(-jnp.abs(x)))


def _gelu_tanh(x):
    return 0.5 * x * (1.0 + jnp.tanh(math.sqrt(2.0 / math.pi) * (x + 0.044715 * (x * x * x))))


def _lam_scalar(lamv):
    s1 = jnp.sum(lamv[0:1] * lamv[1:2], axis=-1, keepdims=True)
    s2 = jnp.sum(lamv[2:3] * lamv[3:4], axis=-1, keepdims=True)
    return jnp.exp(s1) - jnp.exp(s2) + LAM_INIT


class _TokenTiles:
    def __init__(self, n_batch, tiles_per_batch, n_sample_tiles):
        self.tpb = tiles_per_batch
        self.n_prompt = n_batch * tiles_per_batch
        self.n_sample = n_sample_tiles
        self.n_tiles = self.n_prompt + n_sample_tiles

    def prompt_block(self, i):
        blk = (i // self.tpb) * (self.tpb - 1) + jnp.maximum(i % self.tpb - 1, 0)
        return jnp.minimum(blk, self.n_prompt - self.n_prompt // self.tpb - 1)

    def sample_block(self, i):
        return jnp.clip(i - self.n_prompt, 0, self.n_sample - 1)

    def specs(self, d):
        tm = ROW_TILE
        return [pl.BlockSpec((tm, d), lambda i: (0, 0)),
                pl.BlockSpec((tm, d), lambda i: (self.prompt_block(i), 0)),
                pl.BlockSpec((tm, d), lambda i: (self.sample_block(i), 0))]

    def select(self, head_ref, xp_ref, xs_ref):
        i = pl.program_id(0)
        x = jnp.where(i % self.tpb == 0, head_ref[...], xp_ref[...])
        return jnp.where(i >= self.n_prompt, xs_ref[...], x)


def _inproj_body(head_ref, xp_ref, xs_ref, g_ref, w_ref, cs_ref, q_ref, kp_ref, ks_ref, kb_ref, vp_ref, vs_ref, vb_ref,
                 xr_ref, xg_ref, *, tiles):
    x = tiles.select(head_ref, xp_ref, xs_ref)
    is_prompt = pl.program_id(0) < tiles.n_prompt
    ms = jnp.mean(x * x, axis=-1, keepdims=True)
    hn = (x * lax.rsqrt(ms + NORM_EPS) * g_ref[...]).astype(BF16)
    proj = jnp.dot(hn, w_ref[...], preferred_element_type=F32)
    ct, s1, s2 = cs_ref[0], cs_ref[1], cs_ref[2]

    def rope(xc):
        return xc * ct + pltpu.roll(xc, LANES - ROPE_DIM // 2, 1) * s1 + pltpu.roll(xc, ROPE_DIM // 2, 1) * s2

    k_chunks = []
    for c in range(Q_WIDTH // LANES):
        sl = slice(c * LANES, (c + 1) * LANES)
        qc = rope(proj[:, c * LANES:(c + 1) * LANES])
        q_ref[:, sl] = (qc * (HEAD_QK ** -0.5)).astype(BF16)
        kc = rope(proj[:, Q_WIDTH + c * LANES:Q_WIDTH + (c + 1) * LANES])
        k_chunks.append(kc)
        kb_ref[:, sl] = kc.astype(BF16)
    k = jnp.concatenate(k_chunks, axis=1)
    v = proj[:, 2 * Q_WIDTH:2 * Q_WIDTH + ATT_WIDTH]
    vb_ref[...] = v.astype(BF16)

    @pl.when(is_prompt)
    def _():
        kp_ref[...] = k
        vp_ref[...] = v

    @pl.when(jnp.logical_not(is_prompt))
    def _():
        ks_ref[...] = k
        vs_ref[...] = v

    r0 = 2 * Q_WIDTH + ATT_WIDTH
    rw = xr_ref.shape[1]
    xr_ref[...] = proj[:, r0:r0 + rw]
    xg_ref[...] = proj[:, r0 + rw:r0 + 2 * rw]


def _inproj(tiles, head, xp, xs, norm1, w_in_b, cs):
    d = head.shape[1]
    tm = ROW_TILE
    nt = tiles.n_tiles * tm
    rw = (w_in_b.shape[1] - 2 * Q_WIDTH - ATT_WIDTH) // 2
    row = lambda i: (i, 0)
    const = lambda i: (0, 0)
    cs_tile = lambda i: (0, jnp.where(i < tiles.n_prompt, i % tiles.tpb, tiles.tpb), 0)
    npr, nsr = tiles.n_prompt * tm, tiles.n_sample * tm
    prow = lambda i: (jnp.minimum(i, tiles.n_prompt - 1), 0)
    srow = lambda i: (tiles.sample_block(i), 0)
    outs = [
        ((nt, Q_WIDTH), BF16, row),
        ((npr, Q_WIDTH), F32, prow),
        ((nsr, Q_WIDTH), F32, srow),
        ((nt, Q_WIDTH), BF16, row),
        ((npr, ATT_WIDTH), F32, prow),
        ((nsr, ATT_WIDTH), F32, srow),
        ((nt, ATT_WIDTH), BF16, row),
        ((nt, rw), F32, row),
        ((nt, rw), F32, row),
    ]
    return pl.pallas_call(
        functools.partial(_inproj_body, tiles=tiles),
        grid=(tiles.n_tiles,),
        in_specs=tiles.specs(d) + [
            pl.BlockSpec((1, d), const),
            pl.BlockSpec(w_in_b.shape, const),
            pl.BlockSpec((3, tm, LANES), cs_tile),
        ],
        out_specs=[pl.BlockSpec((tm, s[1]), m) for s, _, m in outs],
        out_shape=[jax.ShapeDtypeStruct(s, t) for s, t, _ in outs],
        compiler_params=_params(("arbitrary",)),
        name="inproj",
    )(head, xp, xs, norm1, w_in_b, cs)


def _flash_body(q_ref, k_ref, v_ref, lamv_ref, gain_ref, o_ref, q2_sc, m_sc, acc_sc, *, tq, front):
    qi = pl.program_id(1)
    lane = lax.broadcasted_iota(jnp.int32, (tq, LANES), 1)
    for h in range(ATT_HEADS):
        q = q_ref[:, h * LANES:(h + 1) * LANES]
        zero = jnp.zeros_like(q)
        q2_sc[h] = jnp.concatenate([jnp.where(lane < HEAD_QK, q, zero), jnp.where(lane >= HEAD_QK, q, zero)], axis=0)
    m_sc[...] = jnp.full(m_sc.shape, NEG_INF, F32)
    acc_sc[...] = jnp.zeros(acc_sc.shape, F32)

    def step(ki, width, masked):
        start = pl.multiple_of(ki * tq, tq)
        n_ch = width // LANES
        ones = jnp.ones((width, LANES), BF16)
        for h in range(ATT_HEADS):
            k = k_ref[pl.ds(start, width), h * LANES:(h + 1) * LANES]
            v = v_ref[pl.ds(start, width), h * LANES:(h + 1) * LANES]
            s = lax.dot_general(q2_sc[h], k, (((1,), (1,)), ((), ())), preferred_element_type=F32)
            if masked:
                row = lax.broadcasted_iota(jnp.int32, s.shape, 0)
                col = lax.broadcasted_iota(jnp.int32, s.shape, 1)
                qpos = qi * tq + jnp.where(row >= tq, row - tq, row)
                kpos = ki * tq + col
                s = jnp.where(kpos <= qpos, jnp.where(kpos >= front, s, NEG_INF), NEG_INF)
            chunks = [s[:, c * LANES:(c + 1) * LANES] for c in range(n_ch)]
            m_prev = m_sc[h]
            m_cur = jnp.max(functools.reduce(jnp.maximum, chunks), axis=1, keepdims=True)
            m_next = jnp.maximum(m_prev, m_cur)
            p = jnp.concatenate([jnp.exp(c - m_next) for c in chunks], axis=1).astype(BF16)
            alpha = jnp.exp(m_prev - m_next)
            pv = jnp.dot(p, jnp.concatenate([v, ones], axis=1), preferred_element_type=F32)
            acc_sc[h] = jnp.concatenate([alpha, alpha], axis=1) * acc_sc[h] + pv
            m_sc[h] = m_next

    step(0, tq, True)
    n_plain = jnp.maximum(qi - 1, 0)

    def body(j, carry):
        step(1 + 2 * j, 2 * tq, False)
        return carry

    lax.fori_loop(0, n_plain // 2, body, 0)

    @pl.when(n_plain % 2 == 1)
    def _():
        step(qi - 1, tq, False)

    @pl.when(qi > 0)
    def _():
        step(qi, tq, True)

    lam = _lam_scalar(lamv_ref[...])
    for h in range(ATT_HEADS):
        acc = acc_sc[h]
        o = acc[:, :HEAD_V] / acc[:, HEAD_V:]
        d = o[:tq] - lam * o[tq:]
        ms = jnp.mean(d * d, axis=-1, keepdims=True)
        o_ref[:, h * HEAD_V:(h + 1) * HEAD_V] = (
            d * lax.rsqrt(ms + NORM_EPS) * gain_ref[...] * (1.0 - LAM_INIT)).astype(BF16)


def _prompt_attention(q_all, kb_all, vb_all, lamv, gain, n_batch, lp, front):
    tq = ROW_TILE
    nq = lp // tq
    kern = functools.partial(_flash_body, tq=tq, front=front)
    return pl.pallas_call(
        kern,
        grid=(n_batch, nq),
        in_specs=[
            pl.BlockSpec((tq, Q_WIDTH), lambda b, i: (b * nq + i, 0)),
            pl.BlockSpec((lp, Q_WIDTH), lambda b, i: (b, 0)),
            pl.BlockSpec((lp, ATT_WIDTH), lambda b, i: (b, 0)),
            pl.BlockSpec(lamv.shape, lambda b, i: (0, 0)),
            pl.BlockSpec(gain.shape, lambda b, i: (0, 0)),
        ],
        out_specs=pl.BlockSpec((tq, ATT_WIDTH), lambda b, i: (b * nq + i, 0)),
        out_shape=jax.ShapeDtypeStruct((n_batch * lp, ATT_WIDTH), BF16),
        scratch_shapes=[
            pltpu.VMEM((ATT_HEADS, 2 * tq, LANES), BF16),
            pltpu.VMEM((ATT_HEADS, 2 * tq, LANES), F32),
            pltpu.VMEM((ATT_HEADS, 2 * tq, 2 * HEAD_V), F32),
        ],
        compiler_params=_params(("parallel", "arbitrary")),
        name="prompt_attn",
    )(q_all, kb_all, vb_all, lamv, gain)


def _paged_body(pt_ref, qbd_ref, kn_ref, vn_ref, lamv_ref, gain_ref, *rest, n_pg, n_chunks, n_t):
    del pt_ref
    k_refs = rest[:n_pg]
    v_refs = rest[n_pg:2 * n_pg]
    o_ref = rest[2 * n_pg]
    m_sc, l_sc, acc_sc = rest[2 * n_pg + 1:]
    j = pl.program_id(1)
    n_rows = qbd_ref.shape[0]

    @pl.when(j == 0)
    def _():
        m_sc[...] = jnp.full(m_sc.shape, NEG_INF, F32)
        l_sc[...] = jnp.zeros(l_sc.shape, F32)
        acc_sc[...] = jnp.zeros(acc_sc.shape, F32)

    qbd = qbd_ref[...]
    chunks = [jnp.dot(qbd, k_refs[i][...].astype(BF16), preferred_element_type=F32) for i in range(n_pg)]
    m_prev = m_sc[...]
    m_cur = jnp.max(functools.reduce(jnp.maximum, chunks), axis=1, keepdims=True)
    m_next = jnp.maximum(m_prev, m_cur)
    p = [jnp.exp(c - m_next) for c in chunks]
    alpha = jnp.exp(m_prev - m_next)
    l_sc[...] = alpha * l_sc[...] + jnp.sum(functools.reduce(jnp.add, p), axis=1, keepdims=True)
    page = v_refs[0].shape[0] // ATT_HEADS
    hr = 2 * n_t
    pv = []
    for h in range(ATT_HEADS):
        d = None
        for i in range(n_pg):
            vh = v_refs[i][pl.ds(h, page, stride=ATT_HEADS), :].astype(BF16)
            t = jnp.dot(p[i][h * hr:(h + 1) * hr].astype(BF16), vh, preferred_element_type=F32)
            d = t if d is None else d + t
        pv.append(d)
    acc_sc[...] = acc_sc[...] * alpha + jnp.concatenate(pv, axis=0)
    m_sc[...] = m_next

    @pl.when(j == n_chunks - 1)
    def _():
        qf = qbd.astype(F32)
        kn = kn_ref[...].astype(BF16).astype(F32)
        vn = vn_ref[...].astype(BF16).astype(F32)
        row_t = lax.broadcasted_iota(jnp.int32, (n_rows, 1), 0) % n_t
        s_new = []
        for t2 in range(n_t):
            st = jnp.sum(qf * kn[t2:t2 + 1, :], axis=1, keepdims=True)
            s_new.append(jnp.where(row_t >= t2, st, NEG_INF))
        m_old = m_sc[...]
        m_fin = m_old
        for st in s_new:
            m_fin = jnp.maximum(m_fin, st)
        a_fin = jnp.exp(m_old - m_fin)
        l_fin = a_fin * l_sc[...]
        acc = acc_sc[...] * a_fin
        for t2 in range(n_t):
            pt = jnp.exp(s_new[t2] - m_fin)
            l_fin = l_fin + pt
            v_rows = jnp.concatenate(
                [jnp.broadcast_to(vn[t2:t2 + 1, h * HEAD_V:(h + 1) * HEAD_V], (2 * n_t, HEAD_V))
                 for h in range(ATT_HEADS)], axis=0)
            acc = acc + pt * v_rows
        o = acc / l_fin
        lam = _lam_scalar(lamv_ref[...])
        for h in range(ATT_HEADS):
            r0 = h * 2 * n_t
            d = o[r0:r0 + n_t] - lam * o[r0 + n_t:r0 + 2 * n_t]
            ms = jnp.mean(d * d, axis=-1, keepdims=True)
            o_ref[:, h * HEAD_V:(h + 1) * HEAD_V] = d * lax.rsqrt(ms + NORM_EPS) * gain_ref[...] * (1.0 - LAM_INIT)


def _sample_attention(qbd, k_new, v_new, cache_kt, cache_vm, page_table, lamv, gain):
    n_seq, n_rows, kw = qbd.shape
    n_t = k_new.shape[1]
    n_pages = page_table.shape[1]
    n_pg = math.gcd(n_pages, PAGES_PER_STEP)
    n_chunks = n_pages // n_pg
    pt_flat = page_table.reshape(-1)

    def page_map(i):
        return lambda b, j, pt: (pt[b * n_pages + j * n_pg + i], 0, 0)

    seq3 = lambda b, j, pt: (b, 0, 0)
    const2 = lambda b, j, pt: (0, 0)
    kern = functools.partial(_paged_body, n_pg=n_pg, n_chunks=n_chunks, n_t=n_t)
    grid_spec = pltpu.PrefetchScalarGridSpec(
        num_scalar_prefetch=1,
        grid=(n_seq, n_chunks),
        in_specs=[
            pl.BlockSpec((None, n_rows, kw), seq3),
            pl.BlockSpec((None, n_t, kw), seq3),
            pl.BlockSpec((None, n_t, kw), seq3),
            pl.BlockSpec(lamv.shape, const2),
            pl.BlockSpec(gain.shape, const2),
        ] + [pl.BlockSpec((None,) + cache_kt.shape[1:], page_map(i)) for i in range(n_pg)]
          + [pl.BlockSpec((None,) + cache_vm.shape[1:], page_map(i)) for i in range(n_pg)],
        out_specs=pl.BlockSpec((None, n_t, ATT_WIDTH), seq3),
        scratch_shapes=[
            pltpu.VMEM((n_rows, LANES), F32),
            pltpu.VMEM((n_rows, LANES), F32),
            pltpu.VMEM((n_rows, HEAD_V), F32),
        ],
    )
    return pl.pallas_call(
        kern,
        grid_spec=grid_spec,
        out_shape=jax.ShapeDtypeStruct((n_seq, n_t, ATT_WIDTH), F32),
        compiler_params=_params(("parallel", "arbitrary")),
        name="sample_attn",
    )(pt_flat, qbd, k_new, v_new, lamv, gain, *([cache_kt] * n_pg), *([cache_vm] * n_pg))


def _lru_gates(xc, wa_ref, ba, wi_ref, bi, lam_row):
    half = xc.shape[1] // 2
    xb = xc.astype(BF16)

    def blockdiag(w_ref):
        return jnp.concatenate([
            jnp.dot(xb[:, :half], w_ref[0], preferred_element_type=F32),
            jnp.dot(xb[:, half:], w_ref[1], preferred_element_type=F32)], axis=1)

    gate_r = _sigmoid(blockdiag(wa_ref) + ba)
    gate_i = _sigmoid(blockdiag(wi_ref) + bi)
    log_a = -LRU_C * gate_r * _softplus(-lam_row)
    a = jnp.exp(log_a)
    th = jnp.tanh(log_a)
    om = -2.0 * th / (1.0 - th)
    root = om * lax.rsqrt(jnp.maximum(om, jnp.finfo(jnp.float32).tiny))
    u = root * (gate_i * xc)
    return a, u


def _rglru_prompt_body(xr_ref, xg_ref, cw_ref, cb_ref, wa_ref, ba_ref, wi_ref, bi_ref, lam_ref,
                       rnn_ref, hl_ref, prev_sc, h_sc, *, tc, front):
    c = pl.program_id(1)

    @pl.when(c == 0)
    def _():
        prev_sc[...] = jnp.zeros(prev_sc.shape, F32)
        h_sc[...] = jnp.zeros(h_sc.shape, F32)

    x = xr_ref[...]
    prev = prev_sc[...]
    rw = x.shape[1]
    sub = SUBLANES
    top_row = lax.broadcasted_iota(jnp.int32, (sub, 1), 0)

    def delayed(d):
        r = pltpu.roll(x, d, 0)
        top = jnp.where(top_row < d, pltpu.roll(prev, d, 0), r[0:sub])
        return jnp.concatenate([top, r[sub:]], axis=0)

    cw = cw_ref[...]
    y = cb_ref[...] + delayed(3) * cw[0:1]
    y = y + delayed(2) * cw[1:2]
    y = y + delayed(1) * cw[2:3]
    y = y + x * cw[3:4]
    prev_sc[...] = x[tc - sub:tc]
    a, u = _lru_gates(y, wa_ref, ba_ref[...], wi_ref, bi_ref[...], lam_ref[...])
    row = lax.broadcasted_iota(jnp.int32, (tc, 1), 0)
    u = jnp.where(c * tc + row >= front, u, 0.0)
    a = a.reshape(tc // sub, sub, rw)
    u = u.reshape(tc // sub, sub, rw)
    in_group = lax.broadcasted_iota(jnp.int32, (1, sub, 1), 1)
    d = 1
    while d < sub:
        keep = in_group >= d
        u = jnp.where(keep, a * pltpu.roll(u, d, 1) + u, u)
        a = jnp.where(keep, a * pltpu.roll(a, d, 1), a)
        d *= 2
    carry = h_sc[0:1, :]
    groups = []
    for g in range(tc // sub):
        hg = a[g] * carry + u[g]
        groups.append(hg)
        carry = hg[sub - 1:sub]
    h = jnp.concatenate(groups, axis=0)
    h_sc[0:1, :] = carry
    hl_ref[...] = groups[-1]
    rnn_ref[...] = (_gelu_tanh(xg_ref[...]) * h).astype(BF16)


def _rglru_prompt(xr_all, xg_all, cw, cb, wa_bd, ba, wi_bd, bi, lam, n_batch, lp, front):
    tc = ROW_TILE
    nc = lp // tc
    rw = xr_all.shape[1]
    kern = functools.partial(_rglru_prompt_body, tc=tc, front=front)
    row = lambda b, c: (b * nc + c, 0)
    const2 = lambda b, c: (0, 0)
    const3 = lambda b, c: (0, 0, 0)
    return pl.pallas_call(
        kern,
        grid=(n_batch, nc),
        in_specs=[
            pl.BlockSpec((tc, rw), row),
            pl.BlockSpec((tc, rw), row),
            pl.BlockSpec(cw.shape, const2),
            pl.BlockSpec(cb.shape, const2),
            pl.BlockSpec(wa_bd.shape, const3),
            pl.BlockSpec(ba.shape, const2),
            pl.BlockSpec(wi_bd.shape, const3),
            pl.BlockSpec(bi.shape, const2),
            pl.BlockSpec(lam.shape, const2),
        ],
        out_specs=[
            pl.BlockSpec((tc, rw), row),
            pl.BlockSpec((SUBLANES, rw), lambda b, c: (b, 0)),
        ],
        out_shape=[
            jax.ShapeDtypeStruct((n_batch * lp, rw), BF16),
            jax.ShapeDtypeStruct((n_batch * SUBLANES, rw), F32),
        ],
        scratch_shapes=[pltpu.VMEM((SUBLANES, rw), F32), pltpu.VMEM((SUBLANES, rw), F32)],
        compiler_params=_params(("parallel", "arbitrary")),
        name="rglru_prompt",
    )(xr_all, xg_all, cw, cb, wa_bd, ba, wi_bd, bi, lam)


def _rglru_sample_body(xr_ref, xg_ref, cs_ref, h0_ref, cw_ref, cb_ref, wa_ref, ba_ref, wi_ref, bi_ref, lam_ref,
                       rnn_ref, hl_ref):
    n_t = xr_ref.shape[0]
    n_c = cs_ref.shape[0]
    rows = [cs_ref[i] for i in range(n_c)] + [xr_ref[t] for t in range(n_t)]
    cw = cw_ref[...]
    h = h0_ref[...]
    for t in range(n_t):
        y = cb_ref[...] + rows[t] * cw[0:1]
        for j in range(1, CONV_W):
            y = y + rows[t + j] * cw[j:j + 1]
        a, u = _lru_gates(y, wa_ref, ba_ref[...], wi_ref, bi_ref[...], lam_ref[...])
        h = a * h + u
        rnn_ref[t] = (_gelu_tanh(xg_ref[t]) * h).astype(BF16)
    hl_ref[...] = h


def _rglru_sample(xr_t, xg_t, conv_t, h0, cw, cb, wa_bd, ba, wi_bd, bi, lam):
    n_t, n_seq, rw = xr_t.shape
    return pl.pallas_call(
        _rglru_sample_body,
        out_shape=[
            jax.ShapeDtypeStruct((n_t, n_seq, rw), BF16),
            jax.ShapeDtypeStruct((n_seq, rw), F32),
        ],
        compiler_params=pltpu.CompilerParams(vmem_limit_bytes=VMEM_LIMIT),
        name="rglru_sample",
    )(xr_t, xg_t, conv_t, h0, cw, cb, wa_bd, ba, wi_bd, bi, lam)


def _route(logits):
    g = [logits[i:i + 1] for i in range(N_GROUPS)]
    gmax = functools.reduce(jnp.maximum, g)
    g_idx = jnp.full(gmax.shape, N_GROUPS - 1, jnp.int32)
    for i in range(N_GROUPS - 2, -1, -1):
        g_idx = jnp.where(g[i] == gmax, i, g_idx)
    g_sel = 1.0 / functools.reduce(jnp.add, [jnp.exp(gi - gmax) for gi in g])
    e = []
    for j in range(EXP_PER_GROUP):
        ej = logits[N_GROUPS + j:N_GROUPS + j + 1]
        for grp in range(1, N_GROUPS):
            r = N_GROUPS + grp * EXP_PER_GROUP + j
            ej = jnp.where(g_idx == grp, logits[r:r + 1], ej)
        e.append(ej)
    v1 = functools.reduce(jnp.maximum, e)
    i1 = jnp.full(v1.shape, EXP_PER_GROUP - 1, jnp.int32)
    for j in range(EXP_PER_GROUP - 2, -1, -1):
        i1 = jnp.where(e[j] == v1, j, i1)
    rest = [jnp.where(i1 == j, -jnp.inf, e[j]) for j in range(EXP_PER_GROUP)]
    v2 = functools.reduce(jnp.maximum, rest)
    i2 = jnp.full(v2.shape, EXP_PER_GROUP - 1, jnp.int32)
    for j in range(EXP_PER_GROUP - 2, -1, -1):
        i2 = jnp.where(rest[j] == v2, j, i2)
    z = jnp.exp(v2 - v1)
    w1 = g_sel / (1.0 + z)
    w2 = g_sel * z / (1.0 + z)
    e1 = g_idx * EXP_PER_GROUP + i1
    e2 = g_idx * EXP_PER_GROUP + i2
    rows = [jnp.where(e1 == x, w1, 0.0) + jnp.where(e2 == x, w2, 0.0) for x in range(N_EXPERTS)]
    return jnp.concatenate(rows, axis=0)


def _outproj_body(head_ref, xp_ref, xs_ref, attp_ref, atts_ref, rnnp_ref, rnns_ref, wo_ref, g2_ref, wr_ref, br_ref,
                  h1_ref, hn_ref, gates_ref, *, tiles):
    is_prompt = pl.program_id(0) < tiles.n_prompt
    aw = attp_ref.shape[1]
    att = jnp.where(is_prompt, attp_ref[...], atts_ref[...])
    rnn = jnp.where(is_prompt, rnnp_ref[...], rnns_ref[...])
    mix = jnp.dot(att, wo_ref[0:aw, :], preferred_element_type=F32)
    mix = mix + jnp.dot(rnn, wo_ref[aw:, :], preferred_element_type=F32)
    h1 = tiles.select(head_ref, xp_ref, xs_ref) + mix
    h1_ref[...] = h1
    ms = jnp.mean(h1 * h1, axis=-1, keepdims=True)
    hn = h1 * lax.rsqrt(ms + NORM_EPS) * g2_ref[...]
    hn_hi = hn.astype(BF16)
    hn_ref[...] = hn_hi
    hn_lo = (hn - hn_hi.astype(F32)).astype(BF16)
    lg = jnp.dot(hn_hi, wr_ref[0], preferred_element_type=F32)
    lg = lg + jnp.dot(hn_lo, wr_ref[0], preferred_element_type=F32)
    lg = lg + jnp.dot(hn_hi, wr_ref[1], preferred_element_type=F32)
    logits = jnp.transpose(lg)[0:br_ref.shape[0]] + br_ref[...]
    gates_ref[...] = _route(logits)


def _outproj(tiles, head, xp, xs, att_p, att_s, rnn_p, rnn_s, w_out_b, norm2, w_router, b_router):
    d = head.shape[1]
    tm = ROW_TILE
    nt = tiles.n_tiles * tm
    npt = tiles.n_prompt
    row = lambda i: (i, 0)
    const = lambda i: (0, 0)
    prow = lambda i: (jnp.minimum(i, npt - 1), 0)
    srow = lambda i: (tiles.sample_block(i), 0)
    kern = functools.partial(_outproj_body, tiles=tiles)
    return pl.pallas_call(
        kern,
        grid=(tiles.n_tiles,),
        in_specs=tiles.specs(d) + [
            pl.BlockSpec((tm, att_p.shape[1]), prow),
            pl.BlockSpec((tm, att_s.shape[1]), srow),
            pl.BlockSpec((tm, rnn_p.shape[1]), prow),
            pl.BlockSpec((tm, rnn_s.shape[1]), srow),
            pl.BlockSpec(w_out_b.shape, const),
            pl.BlockSpec(norm2.shape, const),
            pl.BlockSpec(w_router.shape, lambda i: (0, 0, 0)),
            pl.BlockSpec(b_router.shape, const),
        ],
        out_specs=[
            pl.BlockSpec((tm, d), row),
            pl.BlockSpec((tm, d), row),
            pl.BlockSpec((N_EXPERTS, tm), lambda i: (0, i)),
        ],
        out_shape=[
            jax.ShapeDtypeStruct((nt, d), F32),
            jax.ShapeDtypeStruct((nt, d), BF16),
            jax.ShapeDtypeStruct((N_EXPERTS, nt), F32),
        ],
        compiler_params=_params(("parallel",)),
        name="outproj_router",
    )(head, xp, xs, att_p, att_s, rnn_p, rnn_s, w_out_b, norm2, w_router, b_router)


MOE_EXPERTS_PER_STEP = 4


def _moe_body(hn_ref, h1_ref, gates_ref, wg_ref, wu_ref, wd_ref, nf_ref, yp_ref, ys_ref, acc_sc, *, n_prompt_tiles):
    step = pl.program_id(1)

    @pl.when(step == 0)
    def _():
        acc_sc[...] = jnp.zeros(acc_sc.shape, F32)

    x = hn_ref[...]
    gates = gates_ref[...]
    lane = lax.broadcasted_iota(jnp.int32, gates.shape, 1)
    total = None
    for j in range(MOE_EXPERTS_PER_STEP):
        out = None
        half = wg_ref.shape[2] // 2
        for c in range(2):
            cols = slice(c * half, (c + 1) * half)
            hg = jnp.dot(x, wg_ref[j, :, cols], preferred_element_type=F32)
            hu = jnp.dot(x, wu_ref[j, :, cols], preferred_element_type=F32)
            he = (hg * _sigmoid(hg) * hu).astype(BF16)
            part = jnp.dot(he, wd_ref[j, cols, :], preferred_element_type=F32)
            out = part if out is None else out + part
        e = step * MOE_EXPERTS_PER_STEP + j
        gcol = jnp.sum(jnp.where(lane == e, gates, 0.0), axis=1, keepdims=True)
        total = gcol * out if total is None else total + gcol * out
    acc_sc[...] = acc_sc[...] + total

    def final():
        h2 = h1_ref[...] + acc_sc[...]
        ms = jnp.mean(h2 * h2, axis=-1, keepdims=True)
        return h2 * lax.rsqrt(ms + NORM_EPS) * nf_ref[...]

    last = step == pl.num_programs(1) - 1
    is_prompt = pl.program_id(0) < n_prompt_tiles

    @pl.when(last & is_prompt)
    def _():
        yp_ref[...] = final()

    @pl.when(last & jnp.logical_not(is_prompt))
    def _():
        ys_ref[...] = final()


def _moe(hn2, h1, gates, wg_b, wu_b, wd_b, norm_f, np_rows):
    nt, d = h1.shape
    de = wg_b.shape[2]
    tm = next(t for t in (512, ROW_TILE) if np_rows % t == 0 and (nt - np_rows) % t == 0)
    npt, nst = np_rows // tm, (nt - np_rows) // tm
    eps = MOE_EXPERTS_PER_STEP
    row = lambda i, e: (i, 0)
    return pl.pallas_call(
        functools.partial(_moe_body, n_prompt_tiles=npt),
        grid=(nt // tm, N_EXPERTS // eps),
        in_specs=[
            pl.BlockSpec((tm, d), row),
            pl.BlockSpec((tm, d), row),
            pl.BlockSpec((tm, N_EXPERTS), row),
            pl.BlockSpec((eps, d, de), lambda i, e: (e, 0, 0)),
            pl.BlockSpec((eps, d, de), lambda i, e: (e, 0, 0)),
            pl.BlockSpec((eps, de, d), lambda i, e: (e, 0, 0)),
            pl.BlockSpec((1, d), lambda i, e: (0, 0)),
        ],
        out_specs=[
            pl.BlockSpec((tm, d), lambda i, e: (jnp.minimum(i, npt - 1), 0)),
            pl.BlockSpec((tm, d), lambda i, e: (jnp.clip(i - npt, 0, nst - 1), 0)),
        ],
        out_shape=[
            jax.ShapeDtypeStruct((np_rows, d), F32),
            jax.ShapeDtypeStruct((nt - np_rows, d), F32),
        ],
        scratch_shapes=[pltpu.VMEM((tm, d), F32)],
        compiler_params=_params(("arbitrary", "arbitrary")),
        name="moe_dense",
    )(hn2, h1, gates, wg_b, wu_b, wd_b, norm_f)


def _rope_tables(pos):
    half = ROPE_DIM // 2
    inv = jnp.power(ROPE_THETA, -2.0 * jnp.arange(half, dtype=F32) / ROPE_DIM)
    ang = pos[:, None] * inv[None, :]
    cos, sin = jnp.cos(ang), jnp.sin(ang)
    n = pos.shape[0]
    pad = jnp.zeros((n, HEAD_QK - ROPE_DIM), F32)
    ct = jnp.concatenate([cos, cos, pad + 1.0], axis=1)
    s1 = jnp.concatenate([-sin, jnp.zeros_like(sin), pad], axis=1)
    s2 = jnp.concatenate([jnp.zeros_like(sin), sin, pad], axis=1)
    rep = LANES // HEAD_QK
    return jnp.stack([jnp.tile(t, (1, rep)) for t in (ct, s1, s2)])


def _block_diag_halves(w):
    nb, bs, _ = w.shape
    hb = nb // 2
    eye = jnp.eye(hb, dtype=w.dtype)
    halves = [jnp.einsum('nkj,nm->nkmj', w[i * hb:(i + 1) * hb], eye).reshape(hb * bs, hb * bs) for i in range(2)]
    return jnp.stack(halves).astype(BF16)


def kernel(x_prompt, x_sample, cache_k, cache_v, state_conv, state_h, page_table, meta_tokens, norm1, w_in,
           lambda_q1, lambda_k1, lambda_q2, lambda_k2, subln, conv_w, conv_b, rg_w_a, rg_b_a, rg_w_i, rg_b_i,
           lru_lambda, w_out, norm2, w_grp, b_grp, w_rt, b_rt, w_gate, w_up, w_down, norm_f):
    assert w_in.shape[0] == 1, "single-layer trunk only"
    n_batch, seq, d = x_prompt.shape
    n_seq, n_t, _ = x_sample.shape
    assert seq % ROW_TILE == 0 and n_t >= CONV_W - 1
    front = ROW_TILE - N_META
    lp = ROW_TILE + seq
    n_pages, page = page_table.shape[1], cache_k.shape[2]
    past_len = n_pages * page
    ns = n_seq * n_t
    ns_pad = -(-ns // ROW_TILE) * ROW_TILE
    np_rows = n_batch * lp
    rw = lru_lambda.shape[1]

    assert ROW_TILE % n_t == 0
    tiles = _TokenTiles(n_batch, lp // ROW_TILE, ns_pad // ROW_TILE)
    head = jnp.concatenate([jnp.zeros((front, d), F32), meta_tokens.astype(F32)], axis=0)
    xp = x_prompt.reshape(n_batch * seq, d)
    xs = jnp.pad(x_sample.reshape(ns, d), ((0, ns_pad - ns), (0, 0)))
    pos_p = jnp.maximum(jnp.arange(lp) - front, 0)
    pos_s = past_len + jnp.arange(ROW_TILE) % n_t
    cs = _rope_tables(jnp.concatenate([pos_p, pos_s]).astype(F32))

    lamv = jnp.stack([lambda_q1[0], lambda_k1[0], lambda_q2[0], lambda_k2[0]]).astype(F32)
    gain = subln.astype(F32)

    q_all, k_p, k_s, kb_all, v_p, v_s, vb_all, xr_all, xg_all = _inproj(tiles, head, xp, xs, norm1,
                                                                  w_in[0].astype(BF16), cs)

    att_p = _prompt_attention(q_all, kb_all, vb_all, lamv, gain, n_batch, lp, front)
    wa_bd, wi_bd = _block_diag_halves(rg_w_a[0]), _block_diag_halves(rg_w_i[0])
    rnn_w = (conv_w[0], conv_b, wa_bd, rg_b_a, wi_bd, rg_b_i, lru_lambda)
    rnn_p, hl_p = _rglru_prompt(xr_all, xg_all, *rnn_w, n_batch, lp, front)

    sl = slice(np_rows, np_rows + ns)
    q_s = q_all[sl].reshape(n_seq, n_t, ATT_HEADS * 2, HEAD_QK)
    eye = jnp.eye(ATT_HEADS * 2, dtype=BF16)
    qbd = jnp.einsum('bthd,hg->bhtgd', q_s, eye).reshape(n_seq, ATT_HEADS * 2 * n_t, Q_WIDTH)
    att_s = _sample_attention(qbd, k_s[:ns].reshape(n_seq, n_t, Q_WIDTH), v_s[:ns].reshape(n_seq, n_t, ATT_WIDTH),
                              jnp.transpose(cache_k[0], (0, 2, 3, 4, 1)).reshape(-1, Q_WIDTH, page),
                              cache_v[0].reshape(-1, page * ATT_HEADS, HEAD_V),
                              page_table, lamv, gain)
    att_s = jnp.concatenate([att_s.reshape(ns, ATT_WIDTH).astype(BF16),
                             jnp.zeros((ns_pad - ns, ATT_WIDTH), BF16)], axis=0)
    to_tmajor = lambda a: jnp.swapaxes(a[sl].reshape(n_seq, n_t, rw), 0, 1)
    rnn_s, hl_s = _rglru_sample(to_tmajor(xr_all), to_tmajor(xg_all), jnp.swapaxes(state_conv[0], 0, 1),
                                state_h[0].astype(F32), *rnn_w)
    rnn_s = jnp.concatenate([jnp.swapaxes(rnn_s, 0, 1).reshape(ns, rw), jnp.zeros((ns_pad - ns, rw), BF16)], axis=0)

    w_r32 = jnp.concatenate([w_grp[0], w_rt[0], jnp.zeros((d, LANES - N_GROUPS - N_EXPERTS), F32)],
                            axis=1).astype(F32)
    w_r_hi = w_r32.astype(BF16)
    w_router = jnp.stack([w_r_hi, (w_r32 - w_r_hi.astype(F32)).astype(BF16)])
    n_logit = N_GROUPS + N_EXPERTS
    n_logit_rows = -(-n_logit // SUBLANES) * SUBLANES
    b_router = jnp.concatenate([b_grp[0], b_rt[0], jnp.zeros((n_logit_rows - n_logit,), F32)])[:, None]
    h1, hn2, gates_t = _outproj(tiles, head, xp, xs, att_p, att_s, rnn_p, rnn_s, w_out[0].astype(BF16), norm2,
                                w_router, b_router.astype(F32))
    y_p, y_s = _moe(hn2, h1, gates_t.T, w_gate[0].astype(BF16), w_up[0].astype(BF16), w_down[0].astype(BF16),
                    norm_f[None, :], np_rows)

    def prompt_rows(a, skip):
        return a.reshape(n_batch, lp, a.shape[1])[:, skip:]

    y_prompt = prompt_rows(y_p, ROW_TILE)
    y_sample = y_s[:ns].reshape(n_seq, n_t, d)
    k_prompt = prompt_rows(k_p, front).reshape(1, n_batch, N_META + seq, ATT_HEADS, 2, HEAD_QK)
    v_prompt = prompt_rows(v_p, front).reshape(1, n_batch, N_META + seq, ATT_HEADS, HEAD_V)
    k_sample = k_s[:ns].reshape(1, n_seq, n_t, ATT_HEADS, 2, HEAD_QK)
    v_sample = v_s[:ns].reshape(1, n_seq, n_t, ATT_HEADS, HEAD_V)
    conv_prompt = jnp.stack([xr_all[(b + 1) * lp - (CONV_W - 1):(b + 1) * lp] for b in range(n_batch)])[None]
    conv_sample = xr_all[sl].reshape(n_seq, n_t, rw)[:, n_t - (CONV_W - 1):][None]
    h_prompt = hl_p.reshape(n_batch, SUBLANES, rw)[:, SUBLANES - 1][None]
    h_sample = hl_s[None]
    return (y_prompt, y_sample, k_prompt, v_prompt, k_sample, v_sample, conv_prompt, conv_sample,
            h_prompt, h_sample)
```

```python
import functools
import math

import jax
import jax.numpy as jnp
from jax import lax
from jax.experimental import pallas as pl
from jax.experimental.pallas import tpu as pltpu

F32 = jnp.float32
BF16 = jnp.bfloat16

N_META = 16
ATT_HEADS = 4
HEAD_QK = 64
HEAD_V = 2 * HEAD_QK
ATT_WIDTH = ATT_HEADS * HEAD_V
Q_WIDTH = ATT_HEADS * 2 * HEAD_QK
ROPE_DIM = HEAD_QK // 4
ROPE_THETA = 500000.0
RNN_BLOCKS = 8
CONV_W = 4
LRU_C = 8.0
N_GROUPS = 4
EXP_PER_GROUP = 4
N_EXPERTS = N_GROUPS * EXP_PER_GROUP
NORM_EPS = 1e-6
NEG_INF = -1e30
LAM_INIT = 0.8 - 0.6 * math.exp(-0.3 * 0)

LANES = 128
ROW_TILE = 256
VMEM_LIMIT = 56 * 1024 * 1024
SUBLANES = 8
PAGES_PER_STEP = 32


def _params(sem, vmem=VMEM_LIMIT):
    return pltpu.CompilerParams(dimension_semantics=sem, vmem_limit_bytes=vmem)


def _sigmoid(x):
    return 1.0 / (1.0 + jnp.exp(-x))


def _softplus(x):
    return jnp.maximum(x, 0.0) + jnp.log1p(jnp.exp(-jnp.abs(x)))


def _gelu_tanh(x):
    return 0.5 * x * (1.0 + jnp.tanh(math.sqrt(2.0 / math.pi) * (x + 0.044715 * (x * x * x))))


def _lam_scalar(lamv):
    s1 = jnp.sum(lamv[0:1] * lamv[1:2], axis=-1, keepdims=True)
    s2 = jnp.sum(lamv[2:3] * lamv[3:4], axis=-1, keepdims=True)
    return jnp.exp(s1) - jnp.exp(s2) + LAM_INIT


class _TokenTiles:
    def __init__(self, n_batch, tiles_per_batch, n_sample_tiles):
        self.tpb = tiles_per_batch
        self.n_prompt = n_batch * tiles_per_batch
        self.n_sample = n_sample_tiles
        self.n_tiles = self.n_prompt + n_sample_tiles

    def prompt_block(self, i):
        blk = (i // self.tpb) * (self.tpb - 1) + jnp.maximum(i % self.tpb - 1, 0)
        return jnp.minimum(blk, self.n_prompt - self.n_prompt // self.tpb - 1)

    def sample_block(self, i):
        return jnp.clip(i - self.n_prompt, 0, self.n_sample - 1)

    def specs(self, d):
        tm = ROW_TILE
        return [pl.BlockSpec((tm, d), lambda i: (0, 0)),
                pl.BlockSpec((tm, d), lambda i: (self.prompt_block(i), 0)),
                pl.BlockSpec((tm, d), lambda i: (self.sample_block(i), 0))]

    def select(self, head_ref, xp_ref, xs_ref):
        i = pl.program_id(0)
        x = jnp.where(i % self.tpb == 0, head_ref[...], xp_ref[...])
        return jnp.where(i >= self.n_prompt, xs_ref[...], x)


def _inproj_body(head_ref, xp_ref, xs_ref, g_ref, w_ref, cs_ref, q_ref, kp_ref, ks_ref, kb_ref, vp_ref, vs_ref, vb_ref,
                 xr_ref, xg_ref, *, tiles):
    x = tiles.select(head_ref, xp_ref, xs_ref)
    is_prompt = pl.program_id(0) < tiles.n_prompt
    ms = jnp.mean(x * x, axis=-1, keepdims=True)
    hn = (x * lax.rsqrt(ms + NORM_EPS) * g_ref[...]).astype(BF16)
    proj = jnp.dot(hn, w_ref[...], preferred_element_type=F32)
    ct, s1, s2 = cs_ref[0], cs_ref[1], cs_ref[2]

    def rope(xc):
        return xc * ct + pltpu.roll(xc, LANES - ROPE_DIM // 2, 1) * s1 + pltpu.roll(xc, ROPE_DIM // 2, 1) * s2

    k_chunks = []
    for c in range(Q_WIDTH // LANES):
        sl = slice(c * LANES, (c + 1) * LANES)
        qc = rope(proj[:, c * LANES:(c + 1) * LANES])
        q_ref[:, sl] = (qc * (HEAD_QK ** -0.5)).astype(BF16)
        kc = rope(proj[:, Q_WIDTH + c * LANES:Q_WIDTH + (c + 1) * LANES])
        k_chunks.append(kc)
        kb_ref[:, sl] = kc.astype(BF16)
    k = jnp.concatenate(k_chunks, axis=1)
    v = proj[:, 2 * Q_WIDTH:2 * Q_WIDTH + ATT_WIDTH]
    vb_ref[...] = v.astype(BF16)

    @pl.when(is_prompt)
    def _():
        kp_ref[...] = k
        vp_ref[...] = v

    @pl.when(jnp.logical_not(is_prompt))
    def _():
        ks_ref[...] = k
        vs_ref[...] = v

    r0 = 2 * Q_WIDTH + ATT_WIDTH
    rw = xr_ref.shape[1]
    xr_ref[...] = proj[:, r0:r0 + rw]
    xg_ref[...] = proj[:, r0 + rw:r0 + 2 * rw]


def _inproj(tiles, head, xp, xs, norm1, w_in_b, cs):
    d = head.shape[1]
    tm = ROW_TILE
    nt = tiles.n_tiles * tm
    rw = (w_in_b.shape[1] - 2 * Q_WIDTH - ATT_WIDTH) // 2
    row = lambda i: (i, 0)
    const = lambda i: (0, 0)
    cs_tile = lambda i: (0, jnp.where(i < tiles.n_prompt, i % tiles.tpb, tiles.tpb), 0)
    npr, nsr = tiles.n_prompt * tm, tiles.n_sample * tm
    prow = lambda i: (jnp.minimum(i, tiles.n_prompt - 1), 0)
    srow = lambda i: (tiles.sample_block(i), 0)
    outs = [
        ((nt, Q_WIDTH), BF16, row),
        ((npr, Q_WIDTH), F32, prow),
        ((nsr, Q_WIDTH), F32, srow),
        ((nt, Q_WIDTH), BF16, row),
        ((npr, ATT_WIDTH), F32, prow),
        ((nsr, ATT_WIDTH), F32, srow),
        ((nt, ATT_WIDTH), BF16, row),
        ((nt, rw), F32, row),
        ((nt, rw), F32, row),
    ]
    return pl.pallas_call(
        functools.partial(_inproj_body, tiles=tiles),
        grid=(tiles.n_tiles,),
        in_specs=tiles.specs(d) + [
            pl.BlockSpec((1, d), const),
            pl.BlockSpec(w_in_b.shape, const),
            pl.BlockSpec((3, tm, LANES), cs_tile),
        ],
        out_specs=[pl.BlockSpec((tm, s[1]), m) for s, _, m in outs],
        out_shape=[jax.ShapeDtypeStruct(s, t) for s, t, _ in outs],
        compiler_params=_params(("arbitrary",)),
        name="inproj",
    )(head, xp, xs, norm1, w_in_b, cs)


def _flash_body(q_ref, k_ref, v_ref, lamv_ref, gain_ref, o_ref, q2_sc, m_sc, acc_sc, *, tq, front):
    qi = pl.program_id(1)
    lane = lax.broadcasted_iota(jnp.int32, (tq, LANES), 1)
    for h in range(ATT_HEADS):
        q = q_ref[:, h * LANES:(h + 1) * LANES]
        zero = jnp.zeros_like(q)
        q2_sc[h] = jnp.concatenate([jnp.where(lane < HEAD_QK, q, zero), jnp.where(lane >= HEAD_QK, q, zero)], axis=0)
    m_sc[...] = jnp.full(m_sc.shape, NEG_INF, F32)
    acc_sc[...] = jnp.zeros(acc_sc.shape, F32)

    def step(ki, width, masked):
        start = pl.multiple_of(ki * tq, tq)
        n_ch = width // LANES
        ones = jnp.ones((width, LANES), BF16)
        for h in range(ATT_HEADS):
            k = k_ref[pl.ds(start, width), h * LANES:(h + 1) * LANES]
            v = v_ref[pl.ds(start, width), h * LANES:(h + 1) * LANES]
            s = lax.dot_general(q2_sc[h], k, (((1,), (1,)), ((), ())), preferred_element_type=F32)
            if masked:
                row = lax.broadcasted_iota(jnp.int32, s.shape, 0)
                col = lax.broadcasted_iota(jnp.int32, s.shape, 1)
                qpos = qi * tq + jnp.where(row >= tq, row - tq, row)
                kpos = ki * tq + col
                s = jnp.where(kpos <= qpos, jnp.where(kpos >= front, s, NEG_INF), NEG_INF)
            chunks = [s[:, c * LANES:(c + 1) * LANES] for c in range(n_ch)]
            m_prev = m_sc[h]
            m_cur = jnp.max(functools.reduce(jnp.maximum, chunks), axis=1, keepdims=True)
            m_next = jnp.maximum(m_prev, m_cur)
            p = jnp.concatenate([jnp.exp(c - m_next) for c in chunks], axis=1).astype(BF16)
            alpha = jnp.exp(m_prev - m_next)
            pv = jnp.dot(p, jnp.concatenate([v, ones], axis=1), preferred_element_type=F32)
            acc_sc[h] = jnp.concatenate([alpha, alpha], axis=1) * acc_sc[h] + pv
            m_sc[h] = m_next

    step(0, tq, True)
    n_plain = jnp.maximum(qi - 1, 0)

    def body(j, carry):
        step(1 + 2 * j, 2 * tq, False)
        return carry

    lax.fori_loop(0, n_plain // 2, body, 0)

    @pl.when(n_plain % 2 == 1)
    def _():
        step(qi - 1, tq, False)

    @pl.when(qi > 0)
    def _():
        step(qi, tq, True)

    lam = _lam_scalar(lamv_ref[...])
    for h in range(ATT_HEADS):
        acc = acc_sc[h]
        o = acc[:, :HEAD_V] / acc[:, HEAD_V:]
        d = o[:tq] - lam * o[tq:]
        ms = jnp.mean(d * d, axis=-1, keepdims=True)
        o_ref[:, h * HEAD_V:(h + 1) * HEAD_V] = (
            d * lax.rsqrt(ms + NORM_EPS) * gain_ref[...] * (1.0 - LAM_INIT)).astype(BF16)


def _prompt_attention(q_all, kb_all, vb_all, lamv, gain, n_batch, lp, front):
    tq = ROW_TILE
    nq = lp // tq
    kern = functools.partial(_flash_body, tq=tq, front=front)
    return pl.pallas_call(
        kern,
        grid=(n_batch, nq),
        in_specs=[
            pl.BlockSpec((tq, Q_WIDTH), lambda b, i: (b * nq + i, 0)),
            pl.BlockSpec((lp, Q_WIDTH), lambda b, i: (b, 0)),
            pl.BlockSpec((lp, ATT_WIDTH), lambda b, i: (b, 0)),
            pl.BlockSpec(lamv.shape, lambda b, i: (0, 0)),
            pl.BlockSpec(gain.shape, lambda b, i: (0, 0)),
        ],
        out_specs=pl.BlockSpec((tq, ATT_WIDTH), lambda b, i: (b * nq + i, 0)),
        out_shape=jax.ShapeDtypeStruct((n_batch * lp, ATT_WIDTH), BF16),
        scratch_shapes=[
            pltpu.VMEM((ATT_HEADS, 2 * tq, LANES), BF16),
            pltpu.VMEM((ATT_HEADS, 2 * tq, LANES), F32),
            pltpu.VMEM((ATT_HEADS, 2 * tq, 2 * HEAD_V), F32),
        ],
        compiler_params=_params(("parallel", "arbitrary")),
        name="prompt_attn",
    )(q_all, kb_all, vb_all, lamv, gain)


def _paged_body(pt_ref, qbd_ref, kn_ref, vn_ref, lamv_ref, gain_ref, *rest, n_pg, n_chunks, n_t):
    del pt_ref
    k_refs = rest[:n_pg]
    v_refs = rest[n_pg:2 * n_pg]
    o_ref = rest[2 * n_pg]
    m_sc, l_sc, acc_sc = rest[2 * n_pg + 1:]
    j = pl.program_id(1)
    n_rows = qbd_ref.shape[0]

    @pl.when(j == 0)
    def _():
        m_sc[...] = jnp.full(m_sc.shape, NEG_INF, F32)
        l_sc[...] = jnp.zeros(l_sc.shape, F32)
        acc_sc[...] = jnp.zeros(acc_sc.shape, F32)

    qbd = qbd_ref[...]
    chunks = [jnp.dot(qbd, k_refs[i][...].astype(BF16), preferred_element_type=F32) for i in range(n_pg)]
    m_prev = m_sc[...]
    m_cur = jnp.max(functools.reduce(jnp.maximum, chunks), axis=1, keepdims=True)
    m_next = jnp.maximum(m_prev, m_cur)
    p = [jnp.exp(c - m_next) for c in chunks]
    alpha = jnp.exp(m_prev - m_next)
    l_sc[...] = alpha * l_sc[...] + jnp.sum(functools.reduce(jnp.add, p), axis=1, keepdims=True)
    page = v_refs[0].shape[0] // ATT_HEADS
    hr = 2 * n_t
    pv = []
    for h in range(ATT_HEADS):
        d = None
        for i in range(n_pg):
            vh = v_refs[i][pl.ds(h, page, stride=ATT_HEADS), :].astype(BF16)
            t = jnp.dot(p[i][h * hr:(h + 1) * hr].astype(BF16), vh, preferred_element_type=F32)
            d = t if d is None else d + t
        pv.append(d)
    acc_sc[...] = acc_sc[...] * alpha + jnp.concatenate(pv, axis=0)
    m_sc[...] = m_next

    @pl.when(j == n_chunks - 1)
    def _():
        qf = qbd.astype(F32)
        kn = kn_ref[...].astype(BF16).astype(F32)
        vn = vn_ref[...].astype(BF16).astype(F32)
        row_t = lax.broadcasted_iota(jnp.int32, (n_rows, 1), 0) % n_t
        s_new = []
        for t2 in range(n_t):
            st = jnp.sum(qf * kn[t2:t2 + 1, :], axis=1, keepdims=True)
            s_new.append(jnp.where(row_t >= t2, st, NEG_INF))
        m_old = m_sc[...]
        m_fin = m_old
        for st in s_new:
            m_fin = jnp.maximum(m_fin, st)
        a_fin = jnp.exp(m_old - m_fin)
        l_fin = a_fin * l_sc[...]
        acc = acc_sc[...] * a_fin
        for t2 in range(n_t):
            pt = jnp.exp(s_new[t2] - m_fin)
            l_fin = l_fin + pt
            v_rows = jnp.concatenate(
                [jnp.broadcast_to(vn[t2:t2 + 1, h * HEAD_V:(h + 1) * HEAD_V], (2 * n_t, HEAD_V))
                 for h in range(ATT_HEADS)], axis=0)
            acc = acc + pt * v_rows
        o = acc / l_fin
        lam = _lam_scalar(lamv_ref[...])
        for h in range(ATT_HEADS):
            r0 = h * 2 * n_t
            d = o[r0:r0 + n_t] - lam * o[r0 + n_t:r0 + 2 * n_t]
            ms = jnp.mean(d * d, axis=-1, keepdims=True)
            o_ref[:, h * HEAD_V:(h + 1) * HEAD_V] = d * lax.rsqrt(ms + NORM_EPS) * gain_ref[...] * (1.0 - LAM_INIT)


def _sample_attention(qbd, k_new, v_new, cache_kt, cache_vm, page_table, lamv, gain):
    n_seq, n_rows, kw = qbd.shape
    n_t = k_new.shape[1]
    n_pages = page_table.shape[1]
    n_pg = math.gcd(n_pages, PAGES_PER_STEP)
    n_chunks = n_pages // n_pg
    pt_flat = page_table.reshape(-1)

    def page_map(i):
        return lambda b, j, pt: (pt[b * n_pages + j * n_pg + i], 0, 0)

    seq3 = lambda b, j, pt: (b, 0, 0)
    const2 = lambda b, j, pt: (0, 0)
    kern = functools.partial(_paged_body, n_pg=n_pg, n_chunks=n_chunks, n_t=n_t)
    grid_spec = pltpu.PrefetchScalarGridSpec(
        num_scalar_prefetch=1,
        grid=(n_seq, n_chunks),
        in_specs=[
            pl.BlockSpec((None, n_rows, kw), seq3),
            pl.BlockSpec((None, n_t, kw), seq3),
            pl.BlockSpec((None, n_t, kw), seq3),
            pl.BlockSpec(lamv.shape, const2),
            pl.BlockSpec(gain.shape, const2),
        ] + [pl.BlockSpec((None,) + cache_kt.shape[1:], page_map(i)) for i in range(n_pg)]
          + [pl.BlockSpec((None,) + cache_vm.shape[1:], page_map(i)) for i in range(n_pg)],
        out_specs=pl.BlockSpec((None, n_t, ATT_WIDTH), seq3),
        scratch_shapes=[
            pltpu.VMEM((n_rows, LANES), F32),
            pltpu.VMEM((n_rows, LANES), F32),
            pltpu.VMEM((n_rows, HEAD_V), F32),
        ],
    )
    return pl.pallas_call(
        kern,
        grid_spec=grid_spec,
        out_shape=jax.ShapeDtypeStruct((n_seq, n_t, ATT_WIDTH), F32),
        compiler_params=_params(("parallel", "arbitrary")),
        name="sample_attn",
    )(pt_flat, qbd, k_new, v_new, lamv, gain, *([cache_kt] * n_pg), *([cache_vm] * n_pg))


def _lru_gates(xc, wa_ref, ba, wi_ref, bi, lam_row):
    half = xc.shape[1] // 2
    xb = xc.astype(BF16)

    def blockdiag(w_ref):
        return jnp.concatenate([
            jnp.dot(xb[:, :half], w_ref[0], preferred_element_type=F32),
            jnp.dot(xb[:, half:], w_ref[1], preferred_element_type=F32)], axis=1)

    gate_r = _sigmoid(blockdiag(wa_ref) + ba)
    gate_i = _sigmoid(blockdiag(wi_ref) + bi)
    log_a = -LRU_C * gate_r * _softplus(-lam_row)
    a = jnp.exp(log_a)
    th = jnp.tanh(log_a)
    om = -2.0 * th / (1.0 - th)
    root = om * lax.rsqrt(jnp.maximum(om, jnp.finfo(jnp.float32).tiny))
    u = root * (gate_i * xc)
    return a, u


def _rglru_prompt_body(xr_ref, xg_ref, cw_ref, cb_ref, wa_ref, ba_ref, wi_ref, bi_ref, lam_ref,
                       rnn_ref, hl_ref, prev_sc, h_sc, *, tc, front, c=None):
    c = pl.program_id(1) if c is None else c

    @pl.when(c == 0)
    def _():
        prev_sc[...] = jnp.zeros(prev_sc.shape, F32)
        h_sc[...] = jnp.zeros(h_sc.shape, F32)

    x = xr_ref[...]
    prev = prev_sc[...]
    rw = x.shape[1]
    sub = SUBLANES
    top_row = lax.broadcasted_iota(jnp.int32, (sub, 1), 0)

    def delayed(d):
        r = pltpu.roll(x, d, 0)
        top = jnp.where(top_row < d, pltpu.roll(prev, d, 0), r[0:sub])
        return jnp.concatenate([top, r[sub:]], axis=0)

    cw = cw_ref[...]
    y = cb_ref[...] + delayed(3) * cw[0:1]
    y = y + delayed(2) * cw[1:2]
    y = y + delayed(1) * cw[2:3]
    y = y + x * cw[3:4]
    prev_sc[...] = x[tc - sub:tc]
    a, u = _lru_gates(y, wa_ref, ba_ref[...], wi_ref, bi_ref[...], lam_ref[...])
    row = lax.broadcasted_iota(jnp.int32, (tc, 1), 0)
    u = jnp.where(c * tc + row >= front, u, 0.0)
    a = a.reshape(tc // sub, sub, rw)
    u = u.reshape(tc // sub, sub, rw)
    in_group = lax.broadcasted_iota(jnp.int32, (1, sub, 1), 1)
    d = 1
    while d < sub:
        keep = in_group >= d
        u = jnp.where(keep, a * pltpu.roll(u, d, 1) + u, u)
        a = jnp.where(keep, a * pltpu.roll(a, d, 1), a)
        d *= 2
    carry = h_sc[0:1, :]
    groups = []
    for g in range(tc // sub):
        hg = a[g] * carry + u[g]
        groups.append(hg)
        carry = hg[sub - 1:sub]
    h = jnp.concatenate(groups, axis=0)
    h_sc[0:1, :] = carry
    hl_ref[...] = groups[-1]
    rnn_ref[...] = (_gelu_tanh(xg_ref[...]) * h).astype(BF16)


def _rglru_prompt(xr_all, xg_all, cw, cb, wa_bd, ba, wi_bd, bi, lam, n_batch, lp, front):
    tc = ROW_TILE
    nc = lp // tc
    rw = xr_all.shape[1]
    kern = functools.partial(_rglru_prompt_body, tc=tc, front=front)
    row = lambda b, c: (b * nc + c, 0)
    const2 = lambda b, c: (0, 0)
    const3 = lambda b, c: (0, 0, 0)
    return pl.pallas_call(
        kern,
        grid=(n_batch, nc),
        in_specs=[
            pl.BlockSpec((tc, rw), row),
            pl.BlockSpec((tc, rw), row),
            pl.BlockSpec(cw.shape, const2),
            pl.BlockSpec(cb.shape, const2),
            pl.BlockSpec(wa_bd.shape, const3),
            pl.BlockSpec(ba.shape, const2),
            pl.BlockSpec(wi_bd.shape, const3),
            pl.BlockSpec(bi.shape, const2),
            pl.BlockSpec(lam.shape, const2),
        ],
        out_specs=[
            pl.BlockSpec((tc, rw), row),
            pl.BlockSpec((SUBLANES, rw), lambda b, c: (b, 0)),
        ],
        out_shape=[
            jax.ShapeDtypeStruct((n_batch * lp, rw), BF16),
            jax.ShapeDtypeStruct((n_batch * SUBLANES, rw), F32),
        ],
        scratch_shapes=[pltpu.VMEM((SUBLANES, rw), F32), pltpu.VMEM((SUBLANES, rw), F32)],
        compiler_params=_params(("parallel", "arbitrary")),
        name="rglru_prompt",
    )(xr_all, xg_all, cw, cb, wa_bd, ba, wi_bd, bi, lam)


def _rglru_sample_body(xr_ref, xg_ref, cs_ref, h0_ref, cw_ref, cb_ref, wa_ref, ba_ref, wi_ref, bi_ref, lam_ref,
                       rnn_ref, hl_ref):
    n_t = xr_ref.shape[0]
    n_c = cs_ref.shape[0]
    rows = [cs_ref[i] for i in range(n_c)] + [xr_ref[t] for t in range(n_t)]
    cw = cw_ref[...]
    h = h0_ref[...]
    for t in range(n_t):
        y = cb_ref[...] + rows[t] * cw[0:1]
        for j in range(1, CONV_W):
            y = y + rows[t + j] * cw[j:j + 1]
        a, u = _lru_gates(y, wa_ref, ba_ref[...], wi_ref, bi_ref[...], lam_ref[...])
        h = a * h + u
        rnn_ref[t] = (_gelu_tanh(xg_ref[t]) * h).astype(BF16)
    hl_ref[...] = h


def _rglru_sample(xr_t, xg_t, conv_t, h0, cw, cb, wa_bd, ba, wi_bd, bi, lam):
    n_t, n_seq, rw = xr_t.shape
    return pl.pallas_call(
        _rglru_sample_body,
        out_shape=[
            jax.ShapeDtypeStruct((n_t, n_seq, rw), BF16),
            jax.ShapeDtypeStruct((n_seq, rw), F32),
        ],
        compiler_params=pltpu.CompilerParams(vmem_limit_bytes=VMEM_LIMIT),
        name="rglru_sample",
    )(xr_t, xg_t, conv_t, h0, cw, cb, wa_bd, ba, wi_bd, bi, lam)


def _route(logits):
    g = [logits[i:i + 1] for i in range(N_GROUPS)]
    gmax = functools.reduce(jnp.maximum, g)
    g_idx = jnp.full(gmax.shape, N_GROUPS - 1, jnp.int32)
    for i in range(N_GROUPS - 2, -1, -1):
        g_idx = jnp.where(g[i] == gmax, i, g_idx)
    g_sel = 1.0 / functools.reduce(jnp.add, [jnp.exp(gi - gmax) for gi in g])
    e = []
    for j in range(EXP_PER_GROUP):
        ej = logits[N_GROUPS + j:N_GROUPS + j + 1]
        for grp in range(1, N_GROUPS):
            r = N_GROUPS + grp * EXP_PER_GROUP + j
            ej = jnp.where(g_idx == grp, logits[r:r + 1], ej)
        e.append(ej)
    v1 = functools.reduce(jnp.maximum, e)
    i1 = jnp.full(v1.shape, EXP_PER_GROUP - 1, jnp.int32)
    for j in range(EXP_PER_GROUP - 2, -1, -1):
        i1 = jnp.where(e[j] == v1, j, i1)
    rest = [jnp.where(i1 == j, -jnp.inf, e[j]) for j in range(EXP_PER_GROUP)]
    v2 = functools.reduce(jnp.maximum, rest)
    i2 = jnp.full(v2.shape, EXP_PER_GROUP - 1, jnp.int32)
    for j in range(EXP_PER_GROUP - 2, -1, -1):
        i2 = jnp.where(rest[j] == v2, j, i2)
    z = jnp.exp(v2 - v1)
    w1 = g_sel / (1.0 + z)
    w2 = g_sel * z / (1.0 + z)
    e1 = g_idx * EXP_PER_GROUP + i1
    e2 = g_idx * EXP_PER_GROUP + i2
    rows = [jnp.where(e1 == x, w1, 0.0) + jnp.where(e2 == x, w2, 0.0) for x in range(N_EXPERTS)]
    return jnp.concatenate(rows, axis=0)


def _outproj_body(head_ref, xp_ref, xs_ref, attp_ref, atts_ref, xr_ref, xg_ref, cw_ref, cb_ref, wa_ref, ba_ref, wi_ref,
                  bi_ref, lam_ref, rnns_ref, wo_ref, g2_ref, wr_ref, br_ref,
                  h1_ref, hn_ref, gates_ref, hl_ref, rnn_sc, hl_sc, prev_sc, h_sc, *, tiles, front):
    rnnp_refs = (xr_ref, xg_ref, cw_ref, cb_ref, wa_ref, ba_ref, wi_ref, bi_ref, lam_ref)
    is_prompt = pl.program_id(0) < tiles.n_prompt
    aw = attp_ref.shape[1]
    att = jnp.where(is_prompt, attp_ref[...], atts_ref[...])

    _rglru_prompt_body(*rnnp_refs, rnn_sc, hl_sc, prev_sc, h_sc, tc=ROW_TILE, front=front,
                       c=pl.program_id(0) % tiles.tpb)

    @pl.when(is_prompt)
    def _():
        hl_ref[...] = hl_sc[...]

    rnn = jnp.where(is_prompt, rnn_sc[...], rnns_ref[...])
    mix = jnp.dot(att, wo_ref[0:aw, :], preferred_element_type=F32)
    mix = mix + jnp.dot(rnn, wo_ref[aw:, :], preferred_element_type=F32)
    h1 = tiles.select(head_ref, xp_ref, xs_ref) + mix
    h1_ref[...] = h1
    ms = jnp.mean(h1 * h1, axis=-1, keepdims=True)
    hn = h1 * lax.rsqrt(ms + NORM_EPS) * g2_ref[...]
    hn_hi = hn.astype(BF16)
    hn_ref[...] = hn_hi
    hn_lo = (hn - hn_hi.astype(F32)).astype(BF16)
    lg = jnp.dot(hn_hi, wr_ref[0], preferred_element_type=F32)
    lg = lg + jnp.dot(hn_lo, wr_ref[0], preferred_element_type=F32)
    lg = lg + jnp.dot(hn_hi, wr_ref[1], preferred_element_type=F32)
    logits = jnp.transpose(lg)[0:br_ref.shape[0]] + br_ref[...]
    gates_ref[...] = _route(logits)


def _outproj(tiles, head, xp, xs, att_p, att_s, xr_all, xg_all, rnn_w, rnn_s, w_out_b, norm2, w_router, b_router,
             front):
    d = head.shape[1]
    tm = ROW_TILE
    nt = tiles.n_tiles * tm
    npt = tiles.n_prompt
    n_batch = npt // tiles.tpb
    rw = xr_all.shape[1]
    row = lambda i: (i, 0)
    const = lambda i: (0, 0)
    prow = lambda i: (jnp.minimum(i, npt - 1), 0)
    srow = lambda i: (tiles.sample_block(i), 0)
    kern = functools.partial(_outproj_body, tiles=tiles, front=front)
    return pl.pallas_call(
        kern,
        grid=(tiles.n_tiles,),
        in_specs=tiles.specs(d) + [
            pl.BlockSpec((tm, att_p.shape[1]), prow),
            pl.BlockSpec((tm, att_s.shape[1]), srow),
            pl.BlockSpec((tm, rw), row),
            pl.BlockSpec((tm, rw), row),
        ] + [pl.BlockSpec(w.shape, (lambda i: (0, 0, 0)) if w.ndim == 3 else const) for w in rnn_w] + [
            pl.BlockSpec((tm, rnn_s.shape[1]), srow),
            pl.BlockSpec(w_out_b.shape, const),
            pl.BlockSpec(norm2.shape, const),
            pl.BlockSpec(w_router.shape, lambda i: (0, 0, 0)),
            pl.BlockSpec(b_router.shape, const),
        ],
        out_specs=[
            pl.BlockSpec((tm, d), row),
            pl.BlockSpec((tm, d), row),
            pl.BlockSpec((N_EXPERTS, tm), lambda i: (0, i)),
            pl.BlockSpec((SUBLANES, rw), lambda i: (jnp.minimum(i // tiles.tpb, n_batch - 1), 0)),
        ],
        out_shape=[
            jax.ShapeDtypeStruct((nt, d), F32),
            jax.ShapeDtypeStruct((nt, d), BF16),
            jax.ShapeDtypeStruct((N_EXPERTS, nt), F32),
            jax.ShapeDtypeStruct((n_batch * SUBLANES, rw), F32),
        ],
        scratch_shapes=[pltpu.VMEM((tm, rw), BF16)] + [pltpu.VMEM((SUBLANES, rw), F32)] * 3,
        compiler_params=_params(("arbitrary",)),
        name="outproj_router",
    )(head, xp, xs, att_p, att_s, xr_all, xg_all, *rnn_w, rnn_s, w_out_b, norm2, w_router, b_router)


MOE_EXPERTS_PER_STEP = 4


def _moe_body(hn_ref, h1_ref, gates_ref, wg_ref, wu_ref, wd_ref, nf_ref, yp_ref, ys_ref, acc_sc, *, n_prompt_tiles):
    step = pl.program_id(1)

    @pl.when(step == 0)
    def _():
        acc_sc[...] = jnp.zeros(acc_sc.shape, F32)

    x = hn_ref[...]
    gates = gates_ref[...]
    lane = lax.broadcasted_iota(jnp.int32, gates.shape, 1)
    total = None
    for j in range(MOE_EXPERTS_PER_STEP):
        out = None
        half = wg_ref.shape[2] // 2
        for c in range(2):
            cols = slice(c * half, (c + 1) * half)
            hg = jnp.dot(x, wg_ref[j, :, cols], preferred_element_type=F32)
            hu = jnp.dot(x, wu_ref[j, :, cols], preferred_element_type=F32)
            he = (hg * _sigmoid(hg) * hu).astype(BF16)
            part = jnp.dot(he, wd_ref[j, cols, :], preferred_element_type=F32)
            out = part if out is None else out + part
        e = step * MOE_EXPERTS_PER_STEP + j
        gcol = jnp.sum(jnp.where(lane == e, gates, 0.0), axis=1, keepdims=True)
        total = gcol * out if total is None else total + gcol * out
    acc_sc[...] = acc_sc[...] + total

    def final():
        h2 = h1_ref[...] + acc_sc[...]
        ms = jnp.mean(h2 * h2, axis=-1, keepdims=True)
        return h2 * lax.rsqrt(ms + NORM_EPS) * nf_ref[...]

    last = step == pl.num_programs(1) - 1
    is_prompt = pl.program_id(0) < n_prompt_tiles

    @pl.when(last & is_prompt)
    def _():
        yp_ref[...] = final()

    @pl.when(last & jnp.logical_not(is_prompt))
    def _():
        ys_ref[...] = final()


def _moe(hn2, h1, gates, wg_b, wu_b, wd_b, norm_f, np_rows):
    nt, d = h1.shape
    de = wg_b.shape[2]
    tm = next(t for t in (512, ROW_TILE) if np_rows % t == 0 and (nt - np_rows) % t == 0)
    npt, nst = np_rows // tm, (nt - np_rows) // tm
    eps = MOE_EXPERTS_PER_STEP
    row = lambda i, e: (i, 0)
    return pl.pallas_call(
        functools.partial(_moe_body, n_prompt_tiles=npt),
        grid=(nt // tm, N_EXPERTS // eps),
        in_specs=[
            pl.BlockSpec((tm, d), row),
            pl.BlockSpec((tm, d), row),
            pl.BlockSpec((tm, N_EXPERTS), row),
            pl.BlockSpec((eps, d, de), lambda i, e: (e, 0, 0)),
            pl.BlockSpec((eps, d, de), lambda i, e: (e, 0, 0)),
            pl.BlockSpec((eps, de, d), lambda i, e: (e, 0, 0)),
            pl.BlockSpec((1, d), lambda i, e: (0, 0)),
        ],
        out_specs=[
            pl.BlockSpec((tm, d), lambda i, e: (jnp.minimum(i, npt - 1), 0)),
            pl.BlockSpec((tm, d), lambda i, e: (jnp.clip(i - npt, 0, nst - 1), 0)),
        ],
        out_shape=[
            jax.ShapeDtypeStruct((np_rows, d), F32),
            jax.ShapeDtypeStruct((nt - np_rows, d), F32),
        ],
        scratch_shapes=[pltpu.VMEM((tm, d), F32)],
        compiler_params=_params(("arbitrary", "arbitrary")),
        name="moe_dense",
    )(hn2, h1, gates, wg_b, wu_b, wd_b, norm_f)


def _rope_tables(pos):
    half = ROPE_DIM // 2
    inv = jnp.power(ROPE_THETA, -2.0 * jnp.arange(half, dtype=F32) / ROPE_DIM)
    ang = pos[:, None] * inv[None, :]
    cos, sin = jnp.cos(ang), jnp.sin(ang)
    n = pos.shape[0]
    pad = jnp.zeros((n, HEAD_QK - ROPE_DIM), F32)
    ct = jnp.concatenate([cos, cos, pad + 1.0], axis=1)
    s1 = jnp.concatenate([-sin, jnp.zeros_like(sin), pad], axis=1)
    s2 = jnp.concatenate([jnp.zeros_like(sin), sin, pad], axis=1)
    rep = LANES // HEAD_QK
    return jnp.stack([jnp.tile(t, (1, rep)) for t in (ct, s1, s2)])


def _block_diag_halves(w):
    nb, bs, _ = w.shape
    hb = nb // 2
    eye = jnp.eye(hb, dtype=w.dtype)
    halves = [jnp.einsum('nkj,nm->nkmj', w[i * hb:(i + 1) * hb], eye).reshape(hb * bs, hb * bs) for i in range(2)]
    return jnp.stack(halves).astype(BF16)


def kernel(x_prompt, x_sample, cache_k, cache_v, state_conv, state_h, page_table, meta_tokens, norm1, w_in,
           lambda_q1, lambda_k1, lambda_q2, lambda_k2, subln, conv_w, conv_b, rg_w_a, rg_b_a, rg_w_i, rg_b_i,
           lru_lambda, w_out, norm2, w_grp, b_grp, w_rt, b_rt, w_gate, w_up, w_down, norm_f):
    assert w_in.shape[0] == 1, "single-layer trunk only"
    n_batch, seq, d = x_prompt.shape
    n_seq, n_t, _ = x_sample.shape
    assert seq % ROW_TILE == 0 and n_t >= CONV_W - 1
    front = ROW_TILE - N_META
    lp = ROW_TILE + seq
    n_pages, page = page_table.shape[1], cache_k.shape[2]
    past_len = n_pages * page
    ns = n_seq * n_t
    ns_pad = -(-ns // ROW_TILE) * ROW_TILE
    np_rows = n_batch * lp
    rw = lru_lambda.shape[1]

    assert ROW_TILE % n_t == 0
    tiles = _TokenTiles(n_batch, lp // ROW_TILE, ns_pad // ROW_TILE)
    head = jnp.concatenate([jnp.zeros((front, d), F32), meta_tokens.astype(F32)], axis=0)
    xp = x_prompt.reshape(n_batch * seq, d)
    xs = jnp.pad(x_sample.reshape(ns, d), ((0, ns_pad - ns), (0, 0)))
    pos_p = jnp.maximum(jnp.arange(lp) - front, 0)
    pos_s = past_len + jnp.arange(ROW_TILE) % n_t
    cs = _rope_tables(jnp.concatenate([pos_p, pos_s]).astype(F32))

    lamv = jnp.stack([lambda_q1[0], lambda_k1[0], lambda_q2[0], lambda_k2[0]]).astype(F32)
    gain = subln.astype(F32)

    q_all, k_p, k_s, kb_all, v_p, v_s, vb_all, xr_all, xg_all = _inproj(tiles, head, xp, xs, norm1,
                                                                  w_in[0].astype(BF16), cs)

    att_p = _prompt_attention(q_all, kb_all, vb_all, lamv, gain, n_batch, lp, front)
    wa_bd, wi_bd = _block_diag_halves(rg_w_a[0]), _block_diag_halves(rg_w_i[0])
    rnn_w = (conv_w[0], conv_b, wa_bd, rg_b_a, wi_bd, rg_b_i, lru_lambda)

    sl = slice(np_rows, np_rows + ns)
    q_s = q_all[sl].reshape(n_seq, n_t, ATT_HEADS * 2, HEAD_QK)
    eye = jnp.eye(ATT_HEADS * 2, dtype=BF16)
    qbd = jnp.einsum('bthd,hg->bhtgd', q_s, eye).reshape(n_seq, ATT_HEADS * 2 * n_t, Q_WIDTH)
    att_s = _sample_attention(qbd, k_s[:ns].reshape(n_seq, n_t, Q_WIDTH), v_s[:ns].reshape(n_seq, n_t, ATT_WIDTH),
                              jnp.transpose(cache_k[0], (0, 2, 3, 4, 1)).reshape(-1, Q_WIDTH, page),
                              cache_v[0].reshape(-1, page * ATT_HEADS, HEAD_V),
                              page_table, lamv, gain)
    att_s = jnp.concatenate([att_s.reshape(ns, ATT_WIDTH).astype(BF16),
                             jnp.zeros((ns_pad - ns, ATT_WIDTH), BF16)], axis=0)
    to_tmajor = lambda a: jnp.swapaxes(a[sl].reshape(n_seq, n_t, rw), 0, 1)
    rnn_s, hl_s = _rglru_sample(to_tmajor(xr_all), to_tmajor(xg_all), jnp.swapaxes(state_conv[0], 0, 1),
                                state_h[0].astype(F32), *rnn_w)
    rnn_s = jnp.concatenate([jnp.swapaxes(rnn_s, 0, 1).reshape(ns, rw), jnp.zeros((ns_pad - ns, rw), BF16)], axis=0)

    w_r32 = jnp.concatenate([w_grp[0], w_rt[0], jnp.zeros((d, LANES - N_GROUPS - N_EXPERTS), F32)],
                            axis=1).astype(F32)
    w_r_hi = w_r32.astype(BF16)
    w_router = jnp.stack([w_r_hi, (w_r32 - w_r_hi.astype(F32)).astype(BF16)])
    n_logit = N_GROUPS + N_EXPERTS
    n_logit_rows = -(-n_logit // SUBLANES) * SUBLANES
    b_router = jnp.concatenate([b_grp[0], b_rt[0], jnp.zeros((n_logit_rows - n_logit,), F32)])[:, None]
    h1, hn2, gates_t, hl_p = _outproj(tiles, head, xp, xs, att_p, att_s, xr_all, xg_all, rnn_w, rnn_s,
                                      w_out[0].astype(BF16), norm2, w_router, b_router.astype(F32), front)
    y_p, y_s = _moe(hn2, h1, gates_t.T, w_gate[0].astype(BF16), w_up[0].astype(BF16), w_down[0].astype(BF16),
                    norm_f[None, :], np_rows)

    def prompt_rows(a, skip):
        return a.reshape(n_batch, lp, a.shape[1])[:, skip:]

    y_prompt = prompt_rows(y_p, ROW_TILE)
    y_sample = y_s[:ns].reshape(n_seq, n_t, d)
    k_prompt = prompt_rows(k_p, front).reshape(1, n_batch, N_META + seq, ATT_HEADS, 2, HEAD_QK)
    v_prompt = prompt_rows(v_p, front).reshape(1, n_batch, N_META + seq, ATT_HEADS, HEAD_V)
    k_sample = k_s[:ns].reshape(1, n_seq, n_t, ATT_HEADS, 2, HEAD_QK)
    v_sample = v_s[:ns].reshape(1, n_seq, n_t, ATT_HEADS, HEAD_V)
    conv_prompt = jnp.stack([xr_all[(b + 1) * lp - (CONV_W - 1):(b + 1) * lp] for b in range(n_batch)])[None]
    conv_sample = xr_all[sl].reshape(n_seq, n_t, rw)[:, n_t - (CONV_W - 1):][None]
    h_prompt = hl_p.reshape(n_batch, SUBLANES, rw)[:, SUBLANES - 1][None]
    h_sample = hl_s[None]
    return (y_prompt, y_sample, k_prompt, v_prompt, k_sample, v_sample, conv_prompt, conv_sample,
            h_prompt, h_sample)
```
